```python
import jax, jax.numpy as jnp
from jax import lax
import numpy as np

D_MODEL = 1024
BATCH = 8
SEQ = 4096
DEPTH = 2

N_A_LAYERS = DEPTH // 2
N_B_LAYERS = DEPTH - N_A_LAYERS
M_HEADS = 4
M_DV = D_MODEL // M_HEADS
M_DK = M_DV // 2
M_CHUNK = 64
GATE_CAP = 15.0
M_QK_W = M_HEADS * M_DK
M_V_W = M_HEADS * M_DV
M_IN_COLS = 2 * M_QK_W + 2 * M_V_W + 2 * M_HEADS
M_SPLITS = [M_QK_W, 2 * M_QK_W, 2 * M_QK_W + M_V_W, 2 * M_QK_W + 2 * M_V_W, 2 * M_QK_W + 2 * M_V_W + M_HEADS]
A_HEAD_DIM = 64
A_Q_HEADS = D_MODEL // A_HEAD_DIM
A_KV_HEADS = 4
A_GROUP = A_Q_HEADS // A_KV_HEADS
WINDOW = 128
ROPE_DIM = A_HEAD_DIM // 4
ROPE_THETA = 500000.0
D_FF = 4 * D_MODEL
PLE_DIM = 256
LN_EPS = 1e-5
DEEPNORM_ALPHA = (2 * DEPTH) ** 0.25
DEEPNORM_BETA = (8 * DEPTH) ** -0.25

kernel_name = "yoco_mlstm_swa_sink_hybrid"


def layer_norm(x, g, b):
    xf = x.astype(jnp.float32)
    mu = jnp.mean(xf, axis=-1, keepdims=True)
    var = jnp.mean(jnp.square(xf - mu), axis=-1, keepdims=True)
    return ((xf - mu) * lax.rsqrt(var + LN_EPS) * g + b).astype(x.dtype)


def partial_rope(t, pos):
    half = ROPE_DIM // 2
    inv_freq = jnp.power(ROPE_THETA, -jnp.arange(half, dtype=jnp.float32) * (2.0 / ROPE_DIM))
    ang = pos.astype(jnp.float32)[..., None] * inv_freq
    cos = jnp.cos(ang)[:, :, None, :]
    sin = jnp.sin(ang)[:, :, None, :]
    tr = t[..., :ROPE_DIM].astype(jnp.float32)
    t1, t2 = tr[..., :half], tr[..., half:]
    rot = jnp.concatenate([t1 * cos - t2 * sin, t2 * cos + t1 * sin], axis=-1).astype(t.dtype)
    return jnp.concatenate([rot, t[..., ROPE_DIM:]], axis=-1)


def soft_cap(g):
    return GATE_CAP * jnp.tanh(g / GATE_CAP)


def to_chunks(t, nc):
    t = t.reshape(t.shape[0], nc, M_CHUNK, *t.shape[2:])
    return jnp.moveaxis(t, (1, 3), (0, 2))


def mlstm_chunkwise(q, k, v, ig, lf):
    B, S, H, DK = q.shape
    DV = v.shape[-1]
    nc = S // M_CHUNK
    xs = tuple(to_chunks(t, nc) for t in (q, k, v, ig, lf))
    causal = jnp.tril(jnp.ones((M_CHUNK, M_CHUNK), dtype=bool))
    init = (jnp.zeros((B, H, DK, DV), jnp.float32),
            jnp.zeros((B, H, DK), jnp.float32),
            jnp.zeros((B, H), jnp.float32))

    def step(carry, chunk):
        C, n, m = carry
        qc, kc, vc, igc, lfc = chunk
        b = jnp.cumsum(lfc, axis=-1)
        log_d = b[..., :, None] - b[..., None, :] + igc[..., None, :]
        log_d = jnp.where(causal, log_d, -jnp.inf)
        log_inter = b + m[..., None]
        m_t = jnp.maximum(jnp.max(log_d, axis=-1), log_inter)
        d = jnp.exp(log_d - m_t[..., None])
        inter = jnp.exp(log_inter - m_t)
        s = jnp.einsum('bhtk,bhsk->bhts', qc, kc) * d
        num = jnp.einsum('bhts,bhsv->bhtv', s, vc) + inter[..., None] * jnp.einsum('bhtk,bhkv->bhtv', qc, C)
        den = jnp.sum(s, axis=-1) + inter * jnp.einsum('bhtk,bhk->bht', qc, n)
        h = num / jnp.maximum(jnp.abs(den), jnp.exp(-m_t))[..., None]
        b_last = b[..., -1]
        a = b_last[..., None] - b + igc
        m_new = jnp.maximum(b_last + m, jnp.max(a, axis=-1))
        w = jnp.exp(a - m_new[..., None])
        decay = jnp.exp(b_last + m - m_new)
        C_new = decay[..., None, None] * C + jnp.einsum('bhs,bhsk,bhsv->bhkv', w, kc, vc)
        n_new = decay[..., None] * n + jnp.einsum('bhs,bhsk->bhk', w, kc)
        return (C_new, n_new, m_new), h

    _, h = lax.scan(step, init, xs)
    return jnp.moveaxis(h, (0, 2), (1, 3)).reshape(B, S, H, DV)


def mlstm_mixer(x, w_in, b_ig, b_fg, head_g, w_out):
    B, S, _ = x.shape
    proj = x @ w_in
    q, k, v, o, ig, fg = jnp.split(proj, M_SPLITS, axis=-1)
    q = q.reshape(B, S, M_HEADS, M_DK).astype(jnp.float32) * (M_DK ** -0.5)
    k = k.reshape(B, S, M_HEADS, M_DK).astype(jnp.float32)
    v = v.reshape(B, S, M_HEADS, M_DV).astype(jnp.float32)
    log_i = soft_cap(ig.astype(jnp.float32) + b_ig)
    log_f = jax.nn.log_sigmoid(soft_cap(fg.astype(jnp.float32) + b_fg))
    h = mlstm_chunkwise(q, k, v, log_i, log_f)
    mu = jnp.mean(h, axis=-1, keepdims=True)
    var = jnp.mean(jnp.square(h - mu), axis=-1, keepdims=True)
    h = (h - mu) * lax.rsqrt(var + LN_EPS) * head_g.reshape(M_HEADS, M_DV)
    h = h.reshape(B, S, D_MODEL).astype(x.dtype)
    return (jax.nn.sigmoid(o) * h) @ w_out


def shared_kv(x, pos, kv_w, kv_b):
    B, S, _ = x.shape
    kv = x @ kv_w + kv_b
    k, v = jnp.split(kv, 2, axis=-1)
    k = partial_rope(k.reshape(B, S, A_KV_HEADS, A_HEAD_DIM), pos)
    v = v.reshape(B, S, A_KV_HEADS, A_HEAD_DIM)
    return k, v


def banded(t, nb):
    t = t.reshape(t.shape[0], nb, WINDOW, A_KV_HEADS, A_HEAD_DIM)
    prev = jnp.pad(t, ((0, 0), (1, 0), (0, 0), (0, 0), (0, 0)))[:, :-1]
    return jnp.concatenate([prev, t], axis=2)


def swa_sink_attention(x, pos, k_sh, v_sh, w_q, b_q, sinks, w_o, b_o):
    B, S, _ = x.shape
    nb = S // WINDOW
    q = partial_rope((x @ w_q + b_q).reshape(B, S, A_Q_HEADS, A_HEAD_DIM), pos)
    q = q.reshape(B, nb, WINDOW, A_KV_HEADS, A_GROUP, A_HEAD_DIM)
    kb, vb = banded(k_sh, nb), banded(v_sh, nb)
    scores = jnp.einsum('bnqkgd,bnskd->bnkgqs', q, kb).astype(jnp.float32) * (A_HEAD_DIM ** -0.5)
    qi = jnp.arange(WINDOW)[:, None]
    si = jnp.arange(2 * WINDOW)[None, :]
    diff = qi + WINDOW - si
    band = (diff >= 0) & (diff < WINDOW)
    blk = jnp.arange(nb)[:, None, None]
    valid = band[None] & (blk * WINDOW - WINDOW + si[None] >= 0)
    scores = jnp.where(valid[None, :, None, None], scores, -jnp.inf)
    sink = sinks.astype(jnp.float32).reshape(A_KV_HEADS, A_GROUP)[None, None, :, :, None]
    m = jnp.maximum(jnp.max(scores, axis=-1), sink)
    e = jnp.exp(scores - m[..., None])
    probs = e / (jnp.sum(e, axis=-1) + jnp.exp(sink - m))[..., None]
    out = jnp.einsum('bnkgqs,bnskd->bnqkgd', probs.astype(x.dtype), vb).reshape(B, S, D_MODEL)
    return out @ w_o + b_o


def squared_relu_mlp(x, w_up, w_down):
    return jnp.square(jax.nn.relu(x @ w_up)) @ w_down


def setup_inputs(seed: int = 0) -> dict:
    key = jax.random.key(seed)
    ks = jax.random.split(key, 24)
    f32 = jnp.float32
    nrm = lambda k, shape, scale: jax.random.normal(k, shape, f32) * scale
    x = jax.random.normal(ks[0], (BATCH, SEQ, D_MODEL), f32)
    p = jax.random.normal(ks[1], (DEPTH, BATCH, SEQ, PLE_DIM), f32)
    start = jax.random.randint(ks[2], (BATCH, 1), 0, 1024, dtype=jnp.int32)
    positions = (start + jnp.arange(SEQ, dtype=jnp.int32)[None, :]).astype(jnp.int32)
    a_w_in = nrm(ks[3], (N_A_LAYERS, D_MODEL, M_IN_COLS), D_MODEL ** -0.5)
    a_b_igate = nrm(ks[4], (N_A_LAYERS, M_HEADS), 0.1)
    a_b_fgate = jnp.linspace(3.0, 6.0, M_HEADS, dtype=f32)[None, :] + nrm(ks[5], (N_A_LAYERS, M_HEADS), 0.1)
    a_head_norm_g = 1.0 + nrm(ks[6], (N_A_LAYERS, D_MODEL), 0.02)
    a_w_out = nrm(ks[7], (N_A_LAYERS, D_MODEL, D_MODEL), D_MODEL ** -0.5 * DEEPNORM_BETA)
    kv_w = nrm(ks[8], (D_MODEL, 2 * A_KV_HEADS * A_HEAD_DIM), D_MODEL ** -0.5)
    kv_b = nrm(ks[9], (2 * A_KV_HEADS * A_HEAD_DIM,), 0.02)
    b_w_q = nrm(ks[10], (N_B_LAYERS, D_MODEL, A_Q_HEADS * A_HEAD_DIM), D_MODEL ** -0.5)
    b_b_q = nrm(ks[11], (N_B_LAYERS, A_Q_HEADS * A_HEAD_DIM), 0.02)
    b_sinks = nrm(ks[12], (N_B_LAYERS, A_Q_HEADS), 0.5)
    b_w_o = nrm(ks[13], (N_B_LAYERS, A_Q_HEADS * A_HEAD_DIM, D_MODEL), D_MODEL ** -0.5 * DEEPNORM_BETA)
    b_b_o = nrm(ks[14], (N_B_LAYERS, D_MODEL), 0.02)
    mix_ln_g = 1.0 + nrm(ks[15], (DEPTH, D_MODEL), 0.02)
    mix_ln_b = nrm(ks[16], (DEPTH, D_MODEL), 0.02)
    mlp_w_up = nrm(ks[17], (DEPTH, D_MODEL, D_FF), D_MODEL ** -0.5)
    mlp_w_down = nrm(ks[18], (DEPTH, D_FF, D_MODEL), D_FF ** -0.5 * DEEPNORM_BETA)
    mlp_ln_g = 1.0 + nrm(ks[19], (DEPTH, D_MODEL), 0.02)
    mlp_ln_b = nrm(ks[20], (DEPTH, D_MODEL), 0.02)
    ple_w_gate = nrm(ks[21], (DEPTH, D_MODEL, D_MODEL), D_MODEL ** -0.5)
    ple_b_gate = nrm(ks[22], (DEPTH, D_MODEL), 0.02)
    ple_w_proj = nrm(ks[23], (DEPTH, PLE_DIM, D_MODEL), PLE_DIM ** -0.5 * DEEPNORM_BETA)
    return {"x": x, "p": p, "positions": positions,
            "a_w_in": a_w_in, "a_b_igate": a_b_igate, "a_b_fgate": a_b_fgate,
            "a_head_norm_g": a_head_norm_g, "a_w_out": a_w_out,
            "kv_w": kv_w, "kv_b": kv_b,
            "b_w_q": b_w_q, "b_b_q": b_b_q, "b_sinks": b_sinks, "b_w_o": b_w_o, "b_b_o": b_b_o,
            "mix_ln_g": mix_ln_g, "mix_ln_b": mix_ln_b,
            "mlp_w_up": mlp_w_up, "mlp_w_down": mlp_w_down, "mlp_ln_g": mlp_ln_g, "mlp_ln_b": mlp_ln_b,
            "ple_w_gate": ple_w_gate, "ple_b_gate": ple_b_gate, "ple_w_proj": ple_w_proj}


def reference(x, p, positions, a_w_in, a_b_igate, a_b_fgate, a_head_norm_g, a_w_out,
              kv_w, kv_b, b_w_q, b_b_q, b_sinks, b_w_o, b_b_o,
              mix_ln_g, mix_ln_b, mlp_w_up, mlp_w_down, mlp_ln_g, mlp_ln_b,
              ple_w_gate, ple_b_gate, ple_w_proj):
    k_sh, v_sh = None, None
    for i in range(DEPTH):
        if i < N_A_LAYERS:
            mix = mlstm_mixer(x, a_w_in[i], a_b_igate[i], a_b_fgate[i], a_head_norm_g[i], a_w_out[i])
        else:
            if i == N_A_LAYERS:
                k_sh, v_sh = shared_kv(x, positions, kv_w, kv_b)
            j = i - N_A_LAYERS
            mix = swa_sink_attention(x, positions, k_sh, v_sh, b_w_q[j], b_b_q[j], b_sinks[j], b_w_o[j], b_b_o[j])
        x = layer_norm(DEEPNORM_ALPHA * x + mix, mix_ln_g[i], mix_ln_b[i])
        x = layer_norm(DEEPNORM_ALPHA * x + squared_relu_mlp(x, mlp_w_up[i], mlp_w_down[i]), mlp_ln_g[i], mlp_ln_b[i])
        x = x + jax.nn.sigmoid(x @ ple_w_gate[i] + ple_b_gate[i]) * (p[i] @ ple_w_proj[i])
    return x
```

```python
import functools

import jax
import jax.numpy as jnp
from jax import lax
from jax.experimental import pallas as pl
from jax.experimental.pallas import tpu as pltpu

F32 = jnp.float32
BF16 = jnp.bfloat16

D_MODEL = 1024
DEPTH = 2
M_HEADS = 4
M_DV = D_MODEL // M_HEADS
M_DK = M_DV // 2
GATE_CAP = 15.0
A_HEAD_DIM = 64
A_Q_HEADS = D_MODEL // A_HEAD_DIM
A_KV_HEADS = 4
A_GROUP = A_Q_HEADS // A_KV_HEADS
WINDOW = 128
ROPE_DIM = A_HEAD_DIM // 4
ROPE_THETA = 500000.0
D_FF = 4 * D_MODEL
PLE_DIM = 256
LN_EPS = 1e-5
DEEPNORM_ALPHA = (2 * DEPTH) ** 0.25

LANES = 128
VMEM_LIMIT = 56 * 1024 * 1024

FFN_TOKENS = 512
FF_CHUNK = 1024
PROJ_TOKENS = 512
M_BLOCK = 256

_NT = (((1,), (1,)), ((), ()))


def _dot(a, b):
    return jnp.dot(a, b, preferred_element_type=F32)


def _dot_nt(a, b):
    return lax.dot_general(a, b, _NT, preferred_element_type=F32)


def _resident(shape):
    zeros = (0,) * len(shape)
    return pl.BlockSpec(shape, lambda *_: zeros, pipeline_mode=pl.Buffered(1))


def _layer_norm(y, g, b):
    mu = jnp.mean(y, axis=-1, keepdims=True)
    yc = y - mu
    var = jnp.mean(yc * yc, axis=-1, keepdims=True)
    return yc * lax.rsqrt(var + LN_EPS) * g + b


def _ffn_body(a_ref, x_ref, p_ref, wo_ref, bo_ref, g1_ref, b1_ref, wup_ref, wdn_ref,
              g2_ref, b2_ref, wpg_ref, bpg_ref, wpp_ref, o_ref):
    mix = _dot(a_ref[...], wo_ref[...]) + bo_ref[...]
    x1 = _layer_norm(DEEPNORM_ALPHA * x_ref[...] + mix, g1_ref[...], b1_ref[...])
    x1b = x1.astype(BF16)
    acc = None
    for c in range(D_FF // FF_CHUNK):
        cols = slice(c * FF_CHUNK, (c + 1) * FF_CHUNK)
        h = jnp.maximum(_dot(x1b, wup_ref[:, cols]), 0.0)
        d = _dot((h * h).astype(BF16), wdn_ref[cols, :])
        acc = d if acc is None else acc + d
    x2 = _layer_norm(DEEPNORM_ALPHA * x1 + acc, g2_ref[...], b2_ref[...])
    gate = jax.nn.sigmoid(_dot(x2.astype(BF16), wpg_ref[...]) + bpg_ref[...])
    pe = _dot(p_ref[...].astype(BF16), wpp_ref[...])
    o_ref[...] = x2 + gate * pe


def _ffn_call(a, x, p, wo, bo, g1, b1, wup, wdn, g2, b2, wpg, bpg, wpp):
    n, d = x.shape
    tm = FFN_TOKENS
    tok = lambda w: pl.BlockSpec((tm, w), lambda i: (i, 0))
    return pl.pallas_call(
        _ffn_body,
        grid=(n // tm,),
        in_specs=[tok(d), tok(d), tok(PLE_DIM),
                  _resident(wo.shape), _resident(bo.shape), _resident(g1.shape), _resident(b1.shape),
                  _resident(wup.shape), _resident(wdn.shape), _resident(g2.shape), _resident(b2.shape),
                  _resident(wpg.shape), _resident(bpg.shape), _resident(wpp.shape)],
        out_specs=tok(d),
        out_shape=jax.ShapeDtypeStruct((n, d), F32),
        compiler_params=pltpu.CompilerParams(
            dimension_semantics=("parallel",), vmem_limit_bytes=VMEM_LIMIT),
        name="ffn",
    )(a, x, p, wo, bo, g1, b1, wup, wdn, g2, b2, wpg, bpg, wpp)


def _log_sigmoid(z):
    return jnp.minimum(z, 0.0) - jnp.log1p(jnp.exp(-jnp.abs(z)))


def _mlstm_proj_body(x_ref, wq_ref, wkt_ref, wv_ref, wo_ref, wgt_ref, bg_ref,
                     q_ref, kt_ref, v_ref, og_ref, gt_ref):
    xb = x_ref[0].astype(BF16)
    q_ref[0] = (_dot(xb, wq_ref[...]) * (M_DK ** -0.5)).astype(BF16)
    kt_ref[0] = _dot_nt(wkt_ref[...], xb).astype(BF16)
    v_ref[0] = _dot(xb, wv_ref[...]).astype(BF16)
    og_ref[0] = jax.nn.sigmoid(_dot(xb, wo_ref[...])).astype(BF16)
    g = _dot_nt(wgt_ref[...], xb)[0:2 * M_HEADS] + bg_ref[...]
    capped = GATE_CAP * jnp.tanh(g / GATE_CAP)
    row = lax.broadcasted_iota(jnp.int32, capped.shape, 0)
    gt_ref[0] = jnp.where(row < M_HEADS, capped, _log_sigmoid(capped))


def _mlstm_proj_call(x, wq, wkt, wv, wo, wgt, bg):
    b, s, d = x.shape
    t = PROJ_TOKENS
    qk_w = M_HEADS * M_DK
    return pl.pallas_call(
        _mlstm_proj_body,
        grid=(b, s // t),
        in_specs=[pl.BlockSpec((1, t, d), lambda i, j: (i, j, 0)),
                  _resident(wq.shape), _resident(wkt.shape), _resident(wv.shape),
                  _resident(wo.shape), _resident(wgt.shape), _resident(bg.shape)],
        out_specs=[pl.BlockSpec((1, t, qk_w), lambda i, j: (i, j, 0)),
                   pl.BlockSpec((1, qk_w, t), lambda i, j: (i, 0, j)),
                   pl.BlockSpec((1, t, d), lambda i, j: (i, j, 0)),
                   pl.BlockSpec((1, t, d), lambda i, j: (i, j, 0)),
                   pl.BlockSpec((1, 2 * M_HEADS, t), lambda i, j: (i, 0, j))],
        out_shape=[jax.ShapeDtypeStruct((b, s, qk_w), BF16),
                   jax.ShapeDtypeStruct((b, qk_w, s), BF16),
                   jax.ShapeDtypeStruct((b, s, d), BF16),
                   jax.ShapeDtypeStruct((b, s, d), BF16),
                   jax.ShapeDtypeStruct((b, 2 * M_HEADS, s), F32)],
        compiler_params=pltpu.CompilerParams(
            dimension_semantics=("parallel", "parallel"), vmem_limit_bytes=VMEM_LIMIT),
        name="mlstm_proj",
    )(x, wq, wkt, wv, wo, wgt, bg)


def _mlstm_body(q_ref, kt_ref, v_ref, og_ref, gt_ref, gain_ref, o_ref, c_ref, n_ref, m_ref):
    @pl.when(pl.program_id(1) == 0)
    def _():
        c_ref[...] = jnp.zeros_like(c_ref)
        n_ref[...] = jnp.zeros_like(n_ref)
        m_ref[...] = jnp.zeros_like(m_ref)

    L = q_ref.shape[1]
    gt = gt_ref[0]
    log_i, log_f = gt[0:M_HEADS], gt[M_HEADS:2 * M_HEADS]
    row = lax.broadcasted_iota(jnp.int32, (L, L), 0)
    col = lax.broadcasted_iota(jnp.int32, (L, L), 1)
    causal = col <= row
    upper = (row <= col).astype(F32)
    b_rows = jnp.dot(log_f, upper, precision=lax.Precision.HIGHEST,
                     preferred_element_type=F32)
    g_rows = log_i - b_rows
    for h in range(M_HEADS):
        g_row = g_rows[h:h + 1]
        m_prev = m_ref[h]
        gm = jnp.where(causal, g_row, -jnp.inf)
        m_col = jnp.maximum(jnp.max(gm, axis=1, keepdims=True), m_prev)
        d = jnp.exp(gm - m_col)
        b_col = jnp.sum(jnp.where(causal, log_f[h:h + 1], 0.0), axis=1, keepdims=True)
        inter = jnp.exp(m_prev - m_col)
        qh = q_ref[0, :, h * M_DK:(h + 1) * M_DK]
        kth = kt_ref[0, h * M_DK:(h + 1) * M_DK, :]
        vh = v_ref[0, :, h * M_DV:(h + 1) * M_DV]
        c_old = c_ref[h]
        n_old = n_ref[h]
        s = _dot(qh, kth) * d
        num = _dot(s.astype(BF16), vh) + inter * _dot(qh, c_old.astype(BF16))
        den = (jnp.sum(s, axis=1, keepdims=True)
               + inter * jnp.sum(qh.astype(F32) * n_old, axis=1, keepdims=True))
        m_t = b_col + m_col
        hv = num * (1.0 / jnp.maximum(jnp.abs(den), jnp.exp(-m_t)))
        mu = jnp.mean(hv, axis=-1, keepdims=True)
        hc = hv - mu
        var = jnp.mean(hc * hc, axis=-1, keepdims=True)
        hn = hc * lax.rsqrt(var + LN_EPS) * gain_ref[:, h * M_DV:(h + 1) * M_DV]
        og = og_ref[0, :, h * M_DV:(h + 1) * M_DV].astype(F32)
        o_ref[0, :, h * M_DV:(h + 1) * M_DV] = (og * hn).astype(BF16)

        w_rows = d[L - 16:L, :]
        m_last = m_col[L - 1:L, :]
        decay = jnp.exp(m_prev - m_last)
        kw = (kth.astype(F32) * w_rows[15:16, :]).astype(BF16)
        c_ref[h] = decay * c_old + _dot(kw, vh)
        n_ref[h] = decay * n_old + _dot_nt(w_rows.astype(BF16), kth)[15:16, :]
        m_ref[h] = b_col[L - 1:L, :] + m_last


def _mlstm_call(q, kt, v, og, gt, gain):
    b, s, d = v.shape
    L = M_BLOCK
    qk_w = M_HEADS * M_DK
    return pl.pallas_call(
        _mlstm_body,
        grid=(b, s // L),
        in_specs=[pl.BlockSpec((1, L, qk_w), lambda i, j: (i, j, 0)),
                  pl.BlockSpec((1, qk_w, L), lambda i, j: (i, 0, j)),
                  pl.BlockSpec((1, L, d), lambda i, j: (i, j, 0)),
                  pl.BlockSpec((1, L, d), lambda i, j: (i, j, 0)),
                  pl.BlockSpec((1, 2 * M_HEADS, L), lambda i, j: (i, 0, j)),
                  _resident(gain.shape)],
        out_specs=pl.BlockSpec((1, L, d), lambda i, j: (i, j, 0)),
        out_shape=jax.ShapeDtypeStruct((b, s, d), BF16),
        scratch_shapes=[pltpu.VMEM((M_HEADS, M_DK, M_DV), F32),
                        pltpu.VMEM((M_HEADS, 1, M_DK), F32),
                        pltpu.VMEM((M_HEADS, 1, 1), F32)],
        compiler_params=pltpu.CompilerParams(
            dimension_semantics=("parallel", "arbitrary"), vmem_limit_bytes=VMEM_LIMIT),
        name="mlstm",
    )(q, kt, v, og, gt, gain)


def _qkv_body(x_ref, pos_ref, wq_ref, bq_ref, wkv_ref, bkv_ref, invf_ref, q_ref, k_ref, v_ref):
    t = x_ref.shape[0]
    xb = x_ref[...].astype(BF16)
    ang = pos_ref[...].astype(F32) * invf_ref[...]
    cos, sin = jnp.cos(ang), jnp.sin(ang)
    dim = lax.broadcasted_iota(jnp.int32, (t, LANES), 1) & (A_HEAD_DIM - 1)
    half = ROPE_DIM // 2
    c_self = jnp.where(dim < ROPE_DIM, cos, 1.0)
    c_next = jnp.where(dim < half, -sin, 0.0)
    c_prev = jnp.where((dim >= half) & (dim < ROPE_DIM), sin, 0.0)

    def rope(z):
        return (z * c_self + pltpu.roll(z, LANES - half, 1) * c_next
                + pltpu.roll(z, half, 1) * c_prev)

    q = _dot(xb, wq_ref[...]) + bq_ref[...]
    for c in range(q.shape[1] // LANES):
        lanes = slice(c * LANES, (c + 1) * LANES)
        q_ref[:, lanes] = (rope(q[:, lanes]) * (A_HEAD_DIM ** -0.5)).astype(BF16)
    kv = _dot(xb, wkv_ref[...]) + bkv_ref[...]
    kw = k_ref.shape[1]
    for c in range(kw // LANES):
        lanes = slice(c * LANES, (c + 1) * LANES)
        k_ref[:, lanes] = rope(kv[:, lanes]).astype(BF16)
    v_ref[...] = kv[:, kw:].astype(BF16)


def _qkv_call(x, pos, wq, bq, wkv, bkv, invf):
    n, d = x.shape
    t = PROJ_TOKENS
    kw = wkv.shape[1] // 2
    tok = lambda w: pl.BlockSpec((t, w), lambda i: (i, 0))
    return pl.pallas_call(
        _qkv_body,
        grid=(n // t,),
        in_specs=[tok(d), tok(1), _resident(wq.shape), _resident(bq.shape),
                  _resident(wkv.shape), _resident(bkv.shape), _resident(invf.shape)],
        out_specs=[tok(d), tok(kw), tok(kw)],
        out_shape=[jax.ShapeDtypeStruct((n, d), BF16),
                   jax.ShapeDtypeStruct((n, kw), BF16),
                   jax.ShapeDtypeStruct((n, kw), BF16)],
        compiler_params=pltpu.CompilerParams(
            dimension_semantics=("parallel",), vmem_limit_bytes=VMEM_LIMIT),
        name="qkv_rope",
    )(x, pos, wq, bq, wkv, bkv, invf)


def _attn_body(q_ref, kp_ref, kc_ref, vp_ref, vc_ref, sink_ref, o_ref):
    W = WINDOW
    blk = pl.program_id(1)
    rows = A_GROUP * W
    even = lax.broadcasted_iota(jnp.int32, (W, LANES), 1) < A_HEAD_DIM
    qi = lax.broadcasted_iota(jnp.int32, (rows, 2 * W), 0) & (W - 1)
    si = lax.broadcasted_iota(jnp.int32, (rows, 2 * W), 1)
    diff = qi + W - si
    valid = (diff >= 0) & (diff < W) & ((si >= W) | (blk > 0))
    zero = jnp.zeros((W, LANES), BF16)
    for kh in range(A_KV_HEADS):
        kv_lanes = slice(kh * LANES, (kh + 1) * LANES)
        kband = jnp.concatenate([kp_ref[0, :, kv_lanes], kc_ref[0, :, kv_lanes]], axis=0)
        vband = jnp.concatenate([vp_ref[0, :, kv_lanes], vc_ref[0, :, kv_lanes]], axis=0)
        base = kh * A_GROUP * A_HEAD_DIM
        qa = q_ref[0, :, base:base + LANES]
        qb = q_ref[0, :, base + LANES:base + 2 * LANES]
        qs = jnp.concatenate([jnp.where(even, qa, zero), jnp.where(even, zero, qa),
                              jnp.where(even, qb, zero), jnp.where(even, zero, qb)], axis=0)
        sc = jnp.where(valid, _dot_nt(qs, kband), -jnp.inf)
        sink = sink_ref[kh]
        m = jnp.maximum(jnp.max(sc, axis=1, keepdims=True), sink)
        e = jnp.exp(sc - m)
        denom = jnp.sum(e, axis=1, keepdims=True) + jnp.exp(sink - m)
        pr = (e * (1.0 / denom)).astype(BF16)
        r = _dot(pr, vband)
        o_ref[0, :, base:base + LANES] = jnp.where(even, r[0:W], r[W:2 * W]).astype(BF16)
        o_ref[0, :, base + LANES:base + 2 * LANES] = jnp.where(
            even, r[2 * W:3 * W], r[3 * W:4 * W]).astype(BF16)


def _attn_call(q, k, v, sink_cols):
    b, s, d = q.shape
    W = WINDOW
    kw = k.shape[2]
    cur = lambda w: pl.BlockSpec((1, W, w), lambda i, j: (i, j, 0))
    prev = lambda w: pl.BlockSpec((1, W, w), lambda i, j: (i, jnp.maximum(j - 1, 0), 0))
    return pl.pallas_call(
        _attn_body,
        grid=(b, s // W),
        in_specs=[cur(d), prev(kw), cur(kw), prev(kw), cur(kw), _resident(sink_cols.shape)],
        out_specs=cur(d),
        out_shape=jax.ShapeDtypeStruct((b, s, d), BF16),
        compiler_params=pltpu.CompilerParams(
            dimension_semantics=("parallel", "parallel"), vmem_limit_bytes=VMEM_LIMIT),
        name="swa_attn",
    )(q, k, k, v, v, sink_cols)


def _dup_heads(w):
    lead = w.shape[:-1]
    w = w.reshape(*lead, A_KV_HEADS, 1, A_HEAD_DIM)
    return jnp.broadcast_to(w, (*lead, A_KV_HEADS, 2, A_HEAD_DIM)).reshape(*lead, A_KV_HEADS * 2 * A_HEAD_DIM)


def kernel(x, p, positions, a_w_in, a_b_igate, a_b_fgate, a_head_norm_g, a_w_out, kv_w, kv_b, b_w_q, b_b_q, b_sinks, b_w_o, b_b_o, mix_ln_g, mix_ln_b, mlp_w_up, mlp_w_down, mlp_ln_g, mlp_ln_b, ple_w_gate, ple_b_gate, ple_w_proj):
    B, S, D = x.shape
    N = B * S
    row = lambda v: v.reshape(1, -1).astype(F32)

    def ffn(i, a, xs, wo, bo):
        return _ffn_call(a, xs, p[i].reshape(N, PLE_DIM), wo.astype(BF16), row(bo),
                         row(mix_ln_g[i]), row(mix_ln_b[i]),
                         mlp_w_up[i].astype(BF16), mlp_w_down[i].astype(BF16),
                         row(mlp_ln_g[i]), row(mlp_ln_b[i]),
                         ple_w_gate[i].astype(BF16), row(ple_b_gate[i]), ple_w_proj[i].astype(BF16))

    qk_w, v_w = M_HEADS * M_DK, M_HEADS * M_DV
    w_in = a_w_in[0]
    wq = w_in[:, :qk_w].astype(BF16)
    wkt = w_in[:, qk_w:2 * qk_w].T.astype(BF16)
    wv = w_in[:, 2 * qk_w:2 * qk_w + v_w].astype(BF16)
    wo = w_in[:, 2 * qk_w + v_w:2 * qk_w + 2 * v_w].astype(BF16)
    wgt = jnp.pad(w_in[:, 2 * qk_w + 2 * v_w:].T, ((0, 2 * M_HEADS), (0, 0))).astype(BF16)
    bg = jnp.concatenate([a_b_igate[0], a_b_fgate[0]]).reshape(2 * M_HEADS, 1).astype(F32)
    q, kt, v, og, gt = _mlstm_proj_call(x, wq, wkt, wv, wo, wgt, bg)
    hg = _mlstm_call(q, kt, v, og, gt, row(a_head_norm_g[0]))
    xs = ffn(0, hg.reshape(N, D), x.reshape(N, D), a_w_out[0], jnp.zeros((D,), F32))

    kv_half = A_KV_HEADS * A_HEAD_DIM
    wkv = jnp.concatenate([_dup_heads(kv_w[:, :kv_half]), _dup_heads(kv_w[:, kv_half:])], axis=1).astype(BF16)
    bkv = row(jnp.concatenate([_dup_heads(kv_b[:kv_half]), _dup_heads(kv_b[kv_half:])]))
    half = ROPE_DIM // 2
    inv_freq = jnp.power(ROPE_THETA, -jnp.arange(half, dtype=F32) * (2.0 / ROPE_DIM))
    invf = jnp.tile(inv_freq, LANES // half).reshape(1, LANES)
    qr, kr, vr = _qkv_call(xs, positions.reshape(N, 1), b_w_q[0].astype(BF16), row(b_b_q[0]), wkv, bkv, invf)
    sink_cols = jnp.broadcast_to(b_sinks[0].astype(F32).reshape(A_KV_HEADS, A_GROUP, 1, 1),
                                 (A_KV_HEADS, A_GROUP, WINDOW, 1)).reshape(A_KV_HEADS, A_GROUP * WINDOW, 1)
    att = _attn_call(qr.reshape(B, S, D), kr.reshape(B, S, -1), vr.reshape(B, S, -1), sink_cols)
    xs = ffn(1, att.reshape(N, D), xs, b_w_o[0], b_b_o[0])
    return xs.reshape(B, S, D)
```

```python
import functools

import jax
import jax.numpy as jnp
from jax import lax
from jax.experimental import pallas as pl
from jax.experimental.pallas import tpu as pltpu

F32 = jnp.float32
BF16 = jnp.bfloat16

D_MODEL = 1024
DEPTH = 2
M_HEADS = 4
M_DV = D_MODEL // M_HEADS
M_DK = M_DV // 2
GATE_CAP = 15.0
A_HEAD_DIM = 64
A_Q_HEADS = D_MODEL // A_HEAD_DIM
A_KV_HEADS = 4
A_GROUP = A_Q_HEADS // A_KV_HEADS
WINDOW = 128
ROPE_DIM = A_HEAD_DIM // 4
ROPE_THETA = 500000.0
D_FF = 4 * D_MODEL
PLE_DIM = 256
LN_EPS = 1e-5
DEEPNORM_ALPHA = (2 * DEPTH) ** 0.25

LANES = 128
VMEM_LIMIT = 56 * 1024 * 1024

FFN_TOKENS = 512
FF_CHUNK = 1024
PROJ_TOKENS = 512
M_BLOCK = 256
ATTN_QUERIES = 512

_NT = (((1,), (1,)), ((), ()))


def _dot(a, b):
    return jnp.dot(a, b, preferred_element_type=F32)


def _dot_nt(a, b):
    return lax.dot_general(a, b, _NT, preferred_element_type=F32)


def _resident(shape):
    zeros = (0,) * len(shape)
    return pl.BlockSpec(shape, lambda *_: zeros, pipeline_mode=pl.Buffered(1))


def _layer_norm(y, g, b):
    mu = jnp.mean(y, axis=-1, keepdims=True)
    yc = y - mu
    var = jnp.mean(yc * yc, axis=-1, keepdims=True)
    return yc * lax.rsqrt(var + LN_EPS) * g + b


def _ffn_body(a_ref, x_ref, p_ref, wo_ref, bo_ref, g1_ref, b1_ref, wup_ref, wdn_ref,
              g2_ref, b2_ref, wpg_ref, bpg_ref, wpp_ref, o_ref):
    mix = _dot(a_ref[...], wo_ref[...]) + bo_ref[...]
    x1 = _layer_norm(DEEPNORM_ALPHA * x_ref[...] + mix, g1_ref[...], b1_ref[...])
    x1b = x1.astype(BF16)
    acc = None
    for c in range(D_FF // FF_CHUNK):
        cols = slice(c * FF_CHUNK, (c + 1) * FF_CHUNK)
        h = jnp.maximum(_dot(x1b, wup_ref[:, cols]), 0.0)
        d = _dot((h * h).astype(BF16), wdn_ref[cols, :])
        acc = d if acc is None else acc + d
    x2 = _layer_norm(DEEPNORM_ALPHA * x1 + acc, g2_ref[...], b2_ref[...])
    gate = jax.nn.sigmoid(_dot(x2.astype(BF16), wpg_ref[...]) + bpg_ref[...])
    pe = _dot(p_ref[...].astype(BF16), wpp_ref[...])
    o_ref[...] = x2 + gate * pe


def _ffn_call(a, x, p, wo, bo, g1, b1, wup, wdn, g2, b2, wpg, bpg, wpp):
    n, d = x.shape
    tm = FFN_TOKENS
    tok = lambda w: pl.BlockSpec((tm, w), lambda i: (i, 0))
    return pl.pallas_call(
        _ffn_body,
        grid=(n // tm,),
        in_specs=[tok(d), tok(d), tok(PLE_DIM),
                  _resident(wo.shape), _resident(bo.shape), _resident(g1.shape), _resident(b1.shape),
                  _resident(wup.shape), _resident(wdn.shape), _resident(g2.shape), _resident(b2.shape),
                  _resident(wpg.shape), _resident(bpg.shape), _resident(wpp.shape)],
        out_specs=tok(d),
        out_shape=jax.ShapeDtypeStruct((n, d), F32),
        compiler_params=pltpu.CompilerParams(
            dimension_semantics=("parallel",), vmem_limit_bytes=VMEM_LIMIT),
        name="ffn",
    )(a, x, p, wo, bo, g1, b1, wup, wdn, g2, b2, wpg, bpg, wpp)


def _log_sigmoid(z):
    return jnp.minimum(z, 0.0) - jnp.log1p(jnp.exp(-jnp.abs(z)))


def _mlstm_proj_body(x_ref, wq_ref, wkt_ref, wv_ref, wo_ref, wgt_ref, bg_ref,
                     q_ref, kt_ref, v_ref, og_ref, gt_ref):
    xb = x_ref[0].astype(BF16)
    q_ref[0] = (_dot(xb, wq_ref[...]) * (M_DK ** -0.5)).astype(BF16)
    kt_ref[0] = _dot_nt(wkt_ref[...], xb).astype(BF16)
    v_ref[0] = _dot(xb, wv_ref[...]).astype(BF16)
    og_ref[0] = jax.nn.sigmoid(_dot(xb, wo_ref[...])).astype(BF16)
    g = _dot_nt(wgt_ref[...], xb)[0:2 * M_HEADS] + bg_ref[...]
    capped = GATE_CAP * jnp.tanh(g / GATE_CAP)
    row = lax.broadcasted_iota(jnp.int32, capped.shape, 0)
    gt_ref[0] = jnp.where(row < M_HEADS, capped, _log_sigmoid(capped))


def _mlstm_proj_call(x, wq, wkt, wv, wo, wgt, bg):
    b, s, d = x.shape
    t = PROJ_TOKENS
    qk_w = M_HEADS * M_DK
    return pl.pallas_call(
        _mlstm_proj_body,
        grid=(b, s // t),
        in_specs=[pl.BlockSpec((1, t, d), lambda i, j: (i, j, 0)),
                  _resident(wq.shape), _resident(wkt.shape), _resident(wv.shape),
                  _resident(wo.shape), _resident(wgt.shape), _resident(bg.shape)],
        out_specs=[pl.BlockSpec((1, t, qk_w), lambda i, j: (i, j, 0)),
                   pl.BlockSpec((1, qk_w, t), lambda i, j: (i, 0, j)),
                   pl.BlockSpec((1, t, d), lambda i, j: (i, j, 0)),
                   pl.BlockSpec((1, t, d), lambda i, j: (i, j, 0)),
                   pl.BlockSpec((1, 2 * M_HEADS, t), lambda i, j: (i, 0, j))],
        out_shape=[jax.ShapeDtypeStruct((b, s, qk_w), BF16),
                   jax.ShapeDtypeStruct((b, qk_w, s), BF16),
                   jax.ShapeDtypeStruct((b, s, d), BF16),
                   jax.ShapeDtypeStruct((b, s, d), BF16),
                   jax.ShapeDtypeStruct((b, 2 * M_HEADS, s), F32)],
        compiler_params=pltpu.CompilerParams(
            dimension_semantics=("parallel", "parallel"), vmem_limit_bytes=VMEM_LIMIT),
        name="mlstm_proj",
    )(x, wq, wkt, wv, wo, wgt, bg)


def _mlstm_body(q_ref, kt_ref, v_ref, og_ref, gt_ref, gain_ref, o_ref, c_ref, n_ref, m_ref):
    @pl.when(pl.program_id(1) == 0)
    def _():
        c_ref[...] = jnp.zeros_like(c_ref)
        n_ref[...] = jnp.zeros_like(n_ref)
        m_ref[...] = jnp.zeros_like(m_ref)

    L = q_ref.shape[1]
    gt = gt_ref[0]
    log_i, log_f = gt[0:M_HEADS], gt[M_HEADS:2 * M_HEADS]
    row = lax.broadcasted_iota(jnp.int32, (L, L), 0)
    col = lax.broadcasted_iota(jnp.int32, (L, L), 1)
    causal = col <= row
    upper = (row <= col).astype(F32)
    b_rows = jnp.dot(log_f, upper, precision=lax.Precision.HIGHEST,
                     preferred_element_type=F32)
    g_rows = log_i - b_rows
    for h in range(M_HEADS):
        g_row = g_rows[h:h + 1]
        m_prev = m_ref[h]
        gm = jnp.where(causal, g_row, -jnp.inf)
        m_col = jnp.maximum(jnp.max(gm, axis=1, keepdims=True), m_prev)
        d = jnp.exp(gm - m_col)
        b_col = jnp.sum(jnp.where(causal, log_f[h:h + 1], 0.0), axis=1, keepdims=True)
        inter = jnp.exp(m_prev - m_col)
        qh = q_ref[0, :, h * M_DK:(h + 1) * M_DK]
        kth = kt_ref[0, h * M_DK:(h + 1) * M_DK, :]
        vh = v_ref[0, :, h * M_DV:(h + 1) * M_DV]
        c_old = c_ref[h]
        n_old = n_ref[h]
        s = _dot(qh, kth) * d
        num = _dot(s.astype(BF16), vh) + inter * _dot(qh, c_old.astype(BF16))
        den = (jnp.sum(s, axis=1, keepdims=True)
               + inter * jnp.sum(qh.astype(F32) * n_old, axis=1, keepdims=True))
        m_t = b_col + m_col
        hv = num * (1.0 / jnp.maximum(jnp.abs(den), jnp.exp(-m_t)))
        mu = jnp.mean(hv, axis=-1, keepdims=True)
        hc = hv - mu
        var = jnp.mean(hc * hc, axis=-1, keepdims=True)
        hn = hc * lax.rsqrt(var + LN_EPS) * gain_ref[:, h * M_DV:(h + 1) * M_DV]
        og = og_ref[0, :, h * M_DV:(h + 1) * M_DV].astype(F32)
        o_ref[0, :, h * M_DV:(h + 1) * M_DV] = (og * hn).astype(BF16)

        w_rows = d[L - 16:L, :]
        m_last = m_col[L - 1:L, :]
        decay = jnp.exp(m_prev - m_last)
        kw = (kth.astype(F32) * w_rows[15:16, :]).astype(BF16)
        c_ref[h] = decay * c_old + _dot(kw, vh)
        n_ref[h] = decay * n_old + _dot_nt(w_rows.astype(BF16), kth)[15:16, :]
        m_ref[h] = b_col[L - 1:L, :] + m_last


def _mlstm_call(q, kt, v, og, gt, gain):
    b, s, d = v.shape
    L = M_BLOCK
    qk_w = M_HEADS * M_DK
    return pl.pallas_call(
        _mlstm_body,
        grid=(b, s // L),
        in_specs=[pl.BlockSpec((1, L, qk_w), lambda i, j: (i, j, 0)),
                  pl.BlockSpec((1, qk_w, L), lambda i, j: (i, 0, j)),
                  pl.BlockSpec((1, L, d), lambda i, j: (i, j, 0)),
                  pl.BlockSpec((1, L, d), lambda i, j: (i, j, 0)),
                  pl.BlockSpec((1, 2 * M_HEADS, L), lambda i, j: (i, 0, j)),
                  _resident(gain.shape)],
        out_specs=pl.BlockSpec((1, L, d), lambda i, j: (i, j, 0)),
        out_shape=jax.ShapeDtypeStruct((b, s, d), BF16),
        scratch_shapes=[pltpu.VMEM((M_HEADS, M_DK, M_DV), F32),
                        pltpu.VMEM((M_HEADS, 1, M_DK), F32),
                        pltpu.VMEM((M_HEADS, 1, 1), F32)],
        compiler_params=pltpu.CompilerParams(
            dimension_semantics=("parallel", "arbitrary"), vmem_limit_bytes=VMEM_LIMIT),
        name="mlstm",
    )(q, kt, v, og, gt, gain)


def _qkv_body(x_ref, pos_ref, wq_ref, bq_ref, wkv_ref, bkv_ref, invf_ref, q_ref, k_ref, v_ref):
    t = x_ref.shape[0]
    xb = x_ref[...].astype(BF16)
    ang = pos_ref[...].astype(F32) * invf_ref[...]
    cos, sin = jnp.cos(ang), jnp.sin(ang)
    dim = lax.broadcasted_iota(jnp.int32, (t, LANES), 1) & (A_HEAD_DIM - 1)
    half = ROPE_DIM // 2
    c_self = jnp.where(dim < ROPE_DIM, cos, 1.0)
    c_next = jnp.where(dim < half, -sin, 0.0)
    c_prev = jnp.where((dim >= half) & (dim < ROPE_DIM), sin, 0.0)

    def rope(z):
        return (z * c_self + pltpu.roll(z, LANES - half, 1) * c_next
                + pltpu.roll(z, half, 1) * c_prev)

    q = _dot(xb, wq_ref[...]) + bq_ref[...]
    for c in range(q.shape[1] // LANES):
        lanes = slice(c * LANES, (c + 1) * LANES)
        q_ref[:, lanes] = (rope(q[:, lanes]) * (A_HEAD_DIM ** -0.5)).astype(BF16)
    kv = _dot(xb, wkv_ref[...]) + bkv_ref[...]
    low = lax.broadcasted_iota(jnp.int32, (t, LANES), 1) < A_HEAD_DIM

    def dup(z, o_ref, c):
        sw = pltpu.roll(z, A_HEAD_DIM, 1)
        o_ref[:, 2 * c * LANES:(2 * c + 1) * LANES] = jnp.where(low, z, sw).astype(BF16)
        o_ref[:, (2 * c + 1) * LANES:(2 * c + 2) * LANES] = jnp.where(low, sw, z).astype(BF16)

    kv_half = kv.shape[1] // 2
    for c in range(kv_half // LANES):
        dup(rope(kv[:, c * LANES:(c + 1) * LANES]), k_ref, c)
        dup(kv[:, kv_half + c * LANES:kv_half + (c + 1) * LANES], v_ref, c)


def _qkv_call(x, pos, wq, bq, wkv, bkv, invf):
    n, d = x.shape
    t = PROJ_TOKENS
    kw = wkv.shape[1]
    tok = lambda w: pl.BlockSpec((t, w), lambda i: (i, 0))
    return pl.pallas_call(
        _qkv_body,
        grid=(n // t,),
        in_specs=[tok(d), tok(1), _resident(wq.shape), _resident(bq.shape),
                  _resident(wkv.shape), _resident(bkv.shape), _resident(invf.shape)],
        out_specs=[tok(d), tok(kw), tok(kw)],
        out_shape=[jax.ShapeDtypeStruct((n, d), BF16),
                   jax.ShapeDtypeStruct((n, kw), BF16),
                   jax.ShapeDtypeStruct((n, kw), BF16)],
        compiler_params=pltpu.CompilerParams(
            dimension_semantics=("parallel",), vmem_limit_bytes=VMEM_LIMIT),
        name="qkv_rope",
    )(x, pos, wq, bq, wkv, bkv, invf)


def _attn_body(sink_ref, q_ref, kp_ref, kc_ref, vp_ref, vc_ref, o_ref, kf_ref, vf_ref):
    W = WINDOW
    tq = q_ref.shape[1]
    blocks = tq // W
    tile = pl.program_id(1)
    kf_ref[0:W] = kp_ref[0]
    kf_ref[W:] = kc_ref[0]
    vf_ref[0:W] = vp_ref[0]
    vf_ref[W:] = vc_ref[0]
    low = lax.broadcasted_iota(jnp.int32, (2 * W, LANES), 1) < A_HEAD_DIM
    keep_low = jnp.where(low, 1.0, 0.0).astype(BF16)
    keep_high = jnp.where(low, 0.0, 1.0).astype(BF16)
    from_prev = (lax.broadcasted_iota(jnp.int32, (W, W), 1)
                 > lax.broadcasted_iota(jnp.int32, (W, W), 0))
    keep_prev = jnp.where(from_prev, 1.0, 0.0).astype(BF16)
    keep_cur = jnp.where(from_prev, 0.0, 1.0).astype(BF16)
    ones_cols = jnp.concatenate([keep_low, keep_high], axis=0)
    low_out = lax.broadcasted_iota(jnp.int32, (W, LANES), 1) < A_HEAD_DIM
    no_prev = jnp.where(tile > 0, 0.0, -jnp.inf)

    def scores(i, kh):
        kband = kf_ref[i * W:(i + 2) * W, kh * LANES:(kh + 1) * LANES]
        kcat = jnp.concatenate([kband * keep_low, kband * keep_high], axis=0)
        base = kh * A_GROUP * A_HEAD_DIM
        q2 = jnp.concatenate([q_ref[0, i * W:(i + 1) * W, base:base + LANES],
                              q_ref[0, i * W:(i + 1) * W, base + LANES:base + 2 * LANES]], axis=0)
        return _dot_nt(q2, kcat)

    def finish(i, kh, sc):
        vband = vf_ref[i * W:(i + 2) * W, kh * LANES:(kh + 1) * LANES]
        vcat = jnp.concatenate([jnp.concatenate([vband * keep_low, vband * keep_high], axis=0),
                                ones_cols], axis=1)
        base = kh * A_GROUP * A_HEAD_DIM
        slabs, sink_terms = [], []
        for sl in range(2):
            parts, terms = [], []
            for par in range(2):
                s_h = sc[sl * W:(sl + 1) * W, par * 2 * W:(par + 1) * 2 * W]
                s_prev = s_h[:, :W] + no_prev if i == 0 else s_h[:, :W]
                c = jnp.where(from_prev, s_prev, s_h[:, W:])
                m = jnp.max(c, axis=1, keepdims=True)
                e = jnp.exp(c - m).astype(BF16)
                parts += [e * keep_prev, e * keep_cur]
                terms.append(jnp.exp(sink_ref[kh * A_GROUP + 2 * sl + par] - m))
            slabs.append(jnp.concatenate(parts, axis=1))
            sink_terms.append(jnp.where(low_out, terms[0], terms[1]))
        r = _dot(jnp.concatenate(slabs, axis=0), vcat)
        for sl in range(2):
            rows = slice(sl * W, (sl + 1) * W)
            out = r[rows, :LANES] * (1.0 / (r[rows, LANES:] + sink_terms[sl]))
            o_ref[0, i * W:(i + 1) * W, base + sl * LANES:base + (sl + 1) * LANES] = out.astype(BF16)

    units = [(i, kh) for i in range(blocks) for kh in range(A_KV_HEADS)]
    sc = scores(*units[0])
    for u, unit in enumerate(units):
        nxt = scores(*units[u + 1]) if u + 1 < len(units) else None
        finish(*unit, sc)
        sc = nxt


def _attn_call(q, k, v, sinks):
    b, s, d = q.shape
    W = WINDOW
    tq = ATTN_QUERIES
    kw = k.shape[2]
    per = tq // W
    cur = lambda w: pl.BlockSpec((1, tq, w), lambda i, j, *_: (i, j, 0))
    prev = lambda w: pl.BlockSpec((1, W, w), lambda i, j, *_: (i, jnp.maximum(j * per - 1, 0), 0))
    return pl.pallas_call(
        _attn_body,
        grid_spec=pltpu.PrefetchScalarGridSpec(
            num_scalar_prefetch=1,
            grid=(b, s // tq),
            in_specs=[cur(d), prev(kw), cur(kw), prev(kw), cur(kw)],
            out_specs=cur(d),
            scratch_shapes=[pltpu.VMEM((tq + W, kw), BF16), pltpu.VMEM((tq + W, kw), BF16)]),
        out_shape=jax.ShapeDtypeStruct((b, s, d), BF16),
        compiler_params=pltpu.CompilerParams(
            dimension_semantics=("parallel", "parallel"), vmem_limit_bytes=VMEM_LIMIT),
        name="swa_attn",
    )(sinks, q, k, k, v, v)


def kernel(x, p, positions, a_w_in, a_b_igate, a_b_fgate, a_head_norm_g, a_w_out, kv_w, kv_b, b_w_q, b_b_q, b_sinks, b_w_o, b_b_o, mix_ln_g, mix_ln_b, mlp_w_up, mlp_w_down, mlp_ln_g, mlp_ln_b, ple_w_gate, ple_b_gate, ple_w_proj):
    B, S, D = x.shape
    N = B * S
    row = lambda v: v.reshape(1, -1).astype(F32)

    def ffn(i, a, xs, wo, bo):
        return _ffn_call(a, xs, p[i].reshape(N, PLE_DIM), wo.astype(BF16), row(bo),
                         row(mix_ln_g[i]), row(mix_ln_b[i]),
                         mlp_w_up[i].astype(BF16), mlp_w_down[i].astype(BF16),
                         row(mlp_ln_g[i]), row(mlp_ln_b[i]),
                         ple_w_gate[i].astype(BF16), row(ple_b_gate[i]), ple_w_proj[i].astype(BF16))

    qk_w, v_w = M_HEADS * M_DK, M_HEADS * M_DV
    w_in = a_w_in[0]
    wq = w_in[:, :qk_w].astype(BF16)
    wkt = w_in[:, qk_w:2 * qk_w].T.astype(BF16)
    wv = w_in[:, 2 * qk_w:2 * qk_w + v_w].astype(BF16)
    wo = w_in[:, 2 * qk_w + v_w:2 * qk_w + 2 * v_w].astype(BF16)
    wgt = jnp.pad(w_in[:, 2 * qk_w + 2 * v_w:].T, ((0, 2 * M_HEADS), (0, 0))).astype(BF16)
    bg = jnp.concatenate([a_b_igate[0], a_b_fgate[0]]).reshape(2 * M_HEADS, 1).astype(F32)
    q, kt, v, og, gt = _mlstm_proj_call(x, wq, wkt, wv, wo, wgt, bg)
    hg = _mlstm_call(q, kt, v, og, gt, row(a_head_norm_g[0]))
    xs = ffn(0, hg.reshape(N, D), x.reshape(N, D), a_w_out[0], jnp.zeros((D,), F32))

    half = ROPE_DIM // 2
    inv_freq = jnp.power(ROPE_THETA, -jnp.arange(half, dtype=F32) * (2.0 / ROPE_DIM))
    invf = jnp.tile(inv_freq, LANES // half).reshape(1, LANES)
    qr, kr, vr = _qkv_call(xs, positions.reshape(N, 1), b_w_q[0].astype(BF16), row(b_b_q[0]),
                           kv_w.astype(BF16), row(kv_b), invf)
    att = _attn_call(qr.reshape(B, S, D), kr.reshape(B, S, -1), vr.reshape(B, S, -1), b_sinks[0].astype(F32))
    xs = ffn(1, att.reshape(N, D), xs, b_w_o[0], b_b_o[0])
    return xs.reshape(B, S, D)
```

```python
import functools

import jax
import jax.numpy as jnp
from jax import lax
from jax.experimental import pallas as pl
from jax.experimental.pallas import tpu as pltpu

F32 = jnp.float32
BF16 = jnp.bfloat16

D_MODEL = 1024
DEPTH = 2
M_HEADS = 4
M_DV = D_MODEL // M_HEADS
M_DK = M_DV // 2
GATE_CAP = 15.0
A_HEAD_DIM = 64
A_Q_HEADS = D_MODEL // A_HEAD_DIM
A_KV_HEADS = 4
A_GROUP = A_Q_HEADS // A_KV_HEADS
WINDOW = 128
ROPE_DIM = A_HEAD_DIM // 4
ROPE_THETA = 500000.0
D_FF = 4 * D_MODEL
PLE_DIM = 256
LN_EPS = 1e-5
DEEPNORM_ALPHA = (2 * DEPTH) ** 0.25

LANES = 128
VMEM_LIMIT = 56 * 1024 * 1024

FFN_TOKENS = 512
FFN_SUBTILES = 2
FF_CHUNK = 1024
PROJ_TOKENS = 512
M_BLOCK = 256
M_AUG = 16
ATTN_QUERIES = 512

_NT = (((1,), (1,)), ((), ()))


def _dot(a, b):
    return jnp.dot(a, b, preferred_element_type=F32)


def _dot_nt(a, b):
    return lax.dot_general(a, b, _NT, preferred_element_type=F32)


def _resident(shape):
    zeros = (0,) * len(shape)
    return pl.BlockSpec(shape, lambda *_: zeros, pipeline_mode=pl.Buffered(1))


def _layer_norm(y, g, b):
    mu = jnp.mean(y, axis=-1, keepdims=True)
    yc = y - mu
    var = jnp.mean(yc * yc, axis=-1, keepdims=True)
    return yc * lax.rsqrt(var + LN_EPS) * g + b


def _ffn_body(a_ref, x_ref, p_ref, wo_ref, bo_ref, g1_ref, b1_ref, wup_ref, wdn_ref,
              g2_ref, b2_ref, wpg_ref, bpg_ref, wpp_ref, o_ref):
    sub = x_ref.shape[0] // FFN_SUBTILES
    rows = [slice(s * sub, (s + 1) * sub) for s in range(FFN_SUBTILES)]

    def mlp(x1):
        x1b = x1.astype(BF16)
        acc = None
        for c in range(D_FF // FF_CHUNK):
            cols = slice(c * FF_CHUNK, (c + 1) * FF_CHUNK)
            h = jnp.maximum(_dot(x1b, wup_ref[:, cols]), 0.0)
            d = _dot((h * h).astype(BF16), wdn_ref[cols, :])
            acc = d if acc is None else acc + d
        return acc

    mix = [_dot(a_ref[r, :], wo_ref[...]) + bo_ref[...] for r in rows]
    x1 = [_layer_norm(DEEPNORM_ALPHA * x_ref[r, :] + m, g1_ref[...], b1_ref[...]) for r, m in zip(rows, mix)]
    acc = [mlp(v) for v in x1]
    for r, v, a in zip(rows, x1, acc):
        x2 = _layer_norm(DEEPNORM_ALPHA * v + a, g2_ref[...], b2_ref[...])
        gate = jax.nn.sigmoid(_dot(x2.astype(BF16), wpg_ref[...]) + bpg_ref[...])
        pe = _dot(p_ref[r, :].astype(BF16), wpp_ref[...])
        o_ref[r, :] = x2 + gate * pe


def _ffn_call(layer, a, x, p, wo, bo, g1, b1, wup, wdn, g2, b2, wpg, bpg, wpp):
    n, d = x.shape
    tm = FFN_TOKENS
    tok = lambda w: pl.BlockSpec((tm, w), lambda i: (i, 0))
    return pl.pallas_call(
        _ffn_body,
        grid=(n // tm,),
        in_specs=[tok(d), tok(d), pl.BlockSpec((None, tm, PLE_DIM), lambda i: (layer, i, 0)),
                  _resident(wo.shape), _resident(bo.shape), _resident(g1.shape), _resident(b1.shape),
                  _resident(wup.shape), _resident(wdn.shape), _resident(g2.shape), _resident(b2.shape),
                  _resident(wpg.shape), _resident(bpg.shape), _resident(wpp.shape)],
        out_specs=tok(d),
        out_shape=jax.ShapeDtypeStruct((n, d), F32),
        compiler_params=pltpu.CompilerParams(
            dimension_semantics=("parallel",), vmem_limit_bytes=VMEM_LIMIT),
        name="ffn",
    )(a, x, p, wo, bo, g1, b1, wup, wdn, g2, b2, wpg, bpg, wpp)


def _log_sigmoid(z):
    return jnp.minimum(z, 0.0) - jnp.log1p(jnp.exp(-jnp.abs(z)))


def _soft_cap(z):
    return GATE_CAP * jnp.tanh(z / GATE_CAP)


def _mlstm_proj_body(x_ref, wqt_ref, wk_ref, wvt_ref, wot_ref, wgr_ref, bgr_ref,
                     qt_ref, k_ref, vt_ref, ogt_ref, b_ref, cm_ref, gc_ref):
    xb = x_ref[0].astype(BF16)
    xtb = x_ref[0].T.astype(BF16)
    t = xb.shape[0]
    L = M_BLOCK
    z = _soft_cap(_dot(wgr_ref[...], xtb)[0:2 * M_HEADS] + bgr_ref[...])
    ogt_ref[0] = jax.nn.sigmoid(_dot(wot_ref[...], xtb)).astype(BF16)
    log_i = jnp.concatenate([z[:M_HEADS]] * 2, axis=0)
    log_f = _log_sigmoid(jnp.concatenate([z[M_HEADS:]] * 2, axis=0))
    qt_ref[0] = (_dot(wqt_ref[...], xtb) * (M_DK ** -0.5)).astype(BF16)
    vt_ref[0] = _dot(wvt_ref[...], xtb).astype(BF16)
    pos = lax.broadcasted_iota(jnp.int32, log_f.shape, 1) & (L - 1)

    def block_scan(v, combine, identity):
        shift = 1
        while shift < L:
            v = combine(v, jnp.where(pos >= shift, pltpu.roll(v, shift, 1), identity))
            shift *= 2
        return v

    b = block_scan(log_f, jnp.add, 0.0)
    g = log_i - b
    cm = block_scan(g, jnp.maximum, -jnp.inf)
    b_ref[0] = b
    cm_ref[0] = cm
    gc_ref[0] = jnp.concatenate([g, jnp.zeros((LANES - 2 * M_HEADS, t), F32)], axis=0).T
    k_ref[0] = _dot(xb, wk_ref[...]).astype(BF16)


def _mlstm_proj_call(x, wqt, wk, wvt, wot, wgr, bgr):
    b, s, d = x.shape
    t = PROJ_TOKENS
    qk_w = M_HEADS * M_DK
    tok_major = lambda w: pl.BlockSpec((1, t, w), lambda i, j: (i, j, 0))
    feat_major = lambda w: pl.BlockSpec((1, w, t), lambda i, j: (i, 0, j))
    weights = (wqt, wk, wvt, wot, wgr, bgr)
    return pl.pallas_call(
        _mlstm_proj_body,
        grid=(b, s // t),
        in_specs=[tok_major(d)] + [_resident(w.shape) for w in weights],
        out_specs=[feat_major(qk_w), tok_major(qk_w), feat_major(d), feat_major(d),
                   feat_major(2 * M_HEADS), feat_major(2 * M_HEADS), tok_major(LANES)],
        out_shape=[jax.ShapeDtypeStruct((b, qk_w, s), BF16),
                   jax.ShapeDtypeStruct((b, s, qk_w), BF16),
                   jax.ShapeDtypeStruct((b, d, s), BF16),
                   jax.ShapeDtypeStruct((b, d, s), BF16),
                   jax.ShapeDtypeStruct((b, 2 * M_HEADS, s), F32),
                   jax.ShapeDtypeStruct((b, 2 * M_HEADS, s), F32),
                   jax.ShapeDtypeStruct((b, s, LANES), F32)],
        compiler_params=pltpu.CompilerParams(
            dimension_semantics=("parallel", "parallel"), vmem_limit_bytes=VMEM_LIMIT),
        name="mlstm_proj",
    )(x, *weights)


def _mlstm_body(qt_ref, k_ref, vt_ref, ogt_ref, b_ref, cm_ref, gc_ref, gain_ref, o_ref, c_ref, m_ref):
    @pl.when(pl.program_id(1) == 0)
    def _():
        c_ref[...] = jnp.zeros_like(c_ref)
        m_ref[...] = jnp.zeros_like(m_ref)

    L = k_ref.shape[1]
    reps = L // LANES
    keep = (lax.broadcasted_iota(jnp.int32, (L, L), 0)
            <= lax.broadcasted_iota(jnp.int32, (L, L), 1))
    b_rows = b_ref[0]
    cm = cm_ref[0]
    g_cols = gc_ref[0]

    ones_rows = jnp.ones((M_AUG, L), BF16)
    early = []
    for h in range(M_HEADS):
        qt = qt_ref[0, h * M_DK:(h + 1) * M_DK, :]
        kh = k_ref[0, :, h * M_DK:(h + 1) * M_DK]
        lhs = jnp.concatenate([vt_ref[0, h * M_DV:(h + 1) * M_DV, :], ones_rows], axis=0)
        m_prev = m_ref[h]
        m_row = jnp.maximum(cm[h:h + 1], m_prev)
        m_last = m_row[:, L - 1:L]
        g_b = jnp.broadcast_to(g_cols[:, h:h + 1], (L, LANES))
        c_old = c_ref[h]
        st = _dot(kh, qt)
        cq = _dot(c_old.astype(BF16), qt)
        kw = (kh.astype(F32) * jnp.exp(g_b - m_last)).astype(BF16)
        c_ref[h] = jnp.exp(m_prev - m_last) * c_old + _dot(lhs, kw)
        m_ref[h] = b_rows[h:h + 1, L - 1:L] + m_last
        early.append((lhs, m_prev, m_row, g_b, st, cq))

    for h, (lhs, m_prev, m_row, g_b, st, cq) in enumerate(early):
        g_full = jnp.concatenate([g_b] * reps, axis=1)
        d = jnp.where(keep, jnp.exp(g_full - m_row), 0.0)
        tot = _dot(lhs, (st * d).astype(BF16)) + cq * jnp.exp(m_prev - m_row)
        num, den = tot[:M_DV], tot[M_DV:M_DV + 1]
        m_t = b_rows[h:h + 1] + m_row
        dmax = jnp.maximum(jnp.abs(den), jnp.exp(-m_t))
        mu = jnp.mean(num, axis=0, keepdims=True)
        hc = num - mu
        var = jnp.mean(hc * hc, axis=0, keepdims=True)
        hn = hc * lax.rsqrt(var + LN_EPS * dmax * dmax)
        gain = jnp.concatenate([gain_ref[h * M_DV:(h + 1) * M_DV, :]] * reps, axis=1)
        og = ogt_ref[0, h * M_DV:(h + 1) * M_DV, :].astype(F32)
        o_ref[0, :, h * M_DV:(h + 1) * M_DV] = (hn * (gain * og)).T.astype(BF16)


def _mlstm_call(qt, k, vt, ogt, b_rows, cm_rows, g_cols, gain_b):
    b, d, s = vt.shape
    L = M_BLOCK
    qk_w = M_HEADS * M_DK
    tok_major = lambda w: pl.BlockSpec((1, L, w), lambda i, j: (i, j, 0))
    feat_major = lambda w: pl.BlockSpec((1, w, L), lambda i, j: (i, 0, j))
    return pl.pallas_call(
        _mlstm_body,
        grid=(b, s // L),
        in_specs=[feat_major(qk_w), tok_major(qk_w), feat_major(d), feat_major(d),
                  feat_major(2 * M_HEADS), feat_major(2 * M_HEADS), tok_major(LANES),
                  _resident(gain_b.shape)],
        out_specs=tok_major(d),
        out_shape=jax.ShapeDtypeStruct((b, s, d), BF16),
        scratch_shapes=[pltpu.VMEM((M_HEADS, M_DV + M_AUG, M_DK), F32),
                        pltpu.VMEM((M_HEADS, 1, 1), F32)],
        compiler_params=pltpu.CompilerParams(
            dimension_semantics=("parallel", "arbitrary"), vmem_limit_bytes=VMEM_LIMIT),
        name="mlstm",
    )(qt, k, vt, ogt, b_rows, cm_rows, g_cols, gain_b)


def _qkv_body(x_ref, pos_ref, wq_ref, bq_ref, wkv_ref, bkv_ref, invf_ref, q_ref, k_ref, v_ref):
    t = x_ref.shape[0]
    xb = x_ref[...].astype(BF16)
    ang = pos_ref[...].astype(F32) * invf_ref[...]
    cos, sin = jnp.cos(ang), jnp.sin(ang)
    dim = lax.broadcasted_iota(jnp.int32, (t, LANES), 1) & (A_HEAD_DIM - 1)
    half = ROPE_DIM // 2
    c_self = jnp.where(dim < ROPE_DIM, cos, 1.0)
    c_next = jnp.where(dim < half, -sin, 0.0)
    c_prev = jnp.where((dim >= half) & (dim < ROPE_DIM), sin, 0.0)

    def rope(z):
        return (z * c_self + pltpu.roll(z, LANES - half, 1) * c_next
                + pltpu.roll(z, half, 1) * c_prev)

    q = _dot(xb, wq_ref[...]) + bq_ref[...]
    for c in range(q.shape[1] // LANES):
        lanes = slice(c * LANES, (c + 1) * LANES)
        q_ref[:, lanes] = (rope(q[:, lanes]) * (A_HEAD_DIM ** -0.5)).astype(BF16)
    kv = _dot(xb, wkv_ref[...]) + bkv_ref[...]
    low = lax.broadcasted_iota(jnp.int32, (t, LANES), 1) < A_HEAD_DIM

    def dup(z, o_ref, c):
        sw = pltpu.roll(z, A_HEAD_DIM, 1)
        o_ref[:, 2 * c * LANES:(2 * c + 1) * LANES] = jnp.where(low, z, sw).astype(BF16)
        o_ref[:, (2 * c + 1) * LANES:(2 * c + 2) * LANES] = jnp.where(low, sw, z).astype(BF16)

    kv_half = kv.shape[1] // 2
    for c in range(kv_half // LANES):
        dup(rope(kv[:, c * LANES:(c + 1) * LANES]), k_ref, c)
        dup(kv[:, kv_half + c * LANES:kv_half + (c + 1) * LANES], v_ref, c)


def _qkv_call(x, pos, wq, bq, wkv, bkv, invf):
    n, d = x.shape
    t = PROJ_TOKENS
    kw = wkv.shape[1]
    tok = lambda w: pl.BlockSpec((t, w), lambda i: (i, 0))
    return pl.pallas_call(
        _qkv_body,
        grid=(n // t,),
        in_specs=[tok(d), tok(1), _resident(wq.shape), _resident(bq.shape),
                  _resident(wkv.shape), _resident(bkv.shape), _resident(invf.shape)],
        out_specs=[tok(d), tok(kw), tok(kw)],
        out_shape=[jax.ShapeDtypeStruct((n, d), BF16),
                   jax.ShapeDtypeStruct((n, kw), BF16),
                   jax.ShapeDtypeStruct((n, kw), BF16)],
        compiler_params=pltpu.CompilerParams(
            dimension_semantics=("parallel",), vmem_limit_bytes=VMEM_LIMIT),
        name="qkv_rope",
    )(x, pos, wq, bq, wkv, bkv, invf)


def _attn_body(sink_ref, q_ref, kp_ref, kc_ref, vp_ref, vc_ref, o_ref, kf_ref, vf_ref):
    W = WINDOW
    tq = q_ref.shape[1]
    blocks = tq // W
    tile = pl.program_id(1)
    kf_ref[0:W] = kp_ref[0]
    kf_ref[W:] = kc_ref[0]
    vf_ref[0:W] = vp_ref[0]
    vf_ref[W:] = vc_ref[0]
    low = lax.broadcasted_iota(jnp.int32, (2 * W, LANES), 1) < A_HEAD_DIM
    keep_low = jnp.where(low, 1.0, 0.0).astype(BF16)
    keep_high = jnp.where(low, 0.0, 1.0).astype(BF16)
    from_prev = (lax.broadcasted_iota(jnp.int32, (W, W), 1)
                 > lax.broadcasted_iota(jnp.int32, (W, W), 0))
    keep_prev = jnp.where(from_prev, 1.0, 0.0).astype(BF16)
    keep_cur = jnp.where(from_prev, 0.0, 1.0).astype(BF16)
    ones_cols = jnp.concatenate([keep_low, keep_high], axis=0)
    low_out = lax.broadcasted_iota(jnp.int32, (W, LANES), 1) < A_HEAD_DIM
    no_prev = jnp.where(tile > 0, 0.0, -jnp.inf)

    def scores(i, kh):
        kband = kf_ref[i * W:(i + 2) * W, kh * LANES:(kh + 1) * LANES]
        kcat = jnp.concatenate([kband * keep_low, kband * keep_high], axis=0)
        base = kh * A_GROUP * A_HEAD_DIM
        q2 = jnp.concatenate([q_ref[0, i * W:(i + 1) * W, base:base + LANES],
                              q_ref[0, i * W:(i + 1) * W, base + LANES:base + 2 * LANES]], axis=0)
        return _dot_nt(q2, kcat)

    def finish(i, kh, sc):
        vband = vf_ref[i * W:(i + 2) * W, kh * LANES:(kh + 1) * LANES]
        vcat = jnp.concatenate([jnp.concatenate([vband * keep_low, vband * keep_high], axis=0),
                                ones_cols], axis=1)
        base = kh * A_GROUP * A_HEAD_DIM
        slabs, sink_terms = [], []
        for sl in range(2):
            parts, terms = [], []
            for par in range(2):
                s_h = sc[sl * W:(sl + 1) * W, par * 2 * W:(par + 1) * 2 * W]
                s_prev = s_h[:, :W] + no_prev if i == 0 else s_h[:, :W]
                c = jnp.where(from_prev, s_prev, s_h[:, W:])
                m = jnp.max(c, axis=1, keepdims=True)
                e = jnp.exp(c - m).astype(BF16)
                parts += [e * keep_prev, e * keep_cur]
                terms.append(jnp.exp(sink_ref[kh * A_GROUP + 2 * sl + par] - m))
            slabs.append(jnp.concatenate(parts, axis=1))
            sink_terms.append(jnp.where(low_out, terms[0], terms[1]))
        r = _dot(jnp.concatenate(slabs, axis=0), vcat)
        for sl in range(2):
            rows = slice(sl * W, (sl + 1) * W)
            out = r[rows, :LANES] * (1.0 / (r[rows, LANES:] + sink_terms[sl]))
            o_ref[0, i * W:(i + 1) * W, base + sl * LANES:base + (sl + 1) * LANES] = out.astype(BF16)

    units = [(i, kh) for i in range(blocks) for kh in range(A_KV_HEADS)]
    sc = scores(*units[0])
    for u, unit in enumerate(units):
        nxt = scores(*units[u + 1]) if u + 1 < len(units) else None
        finish(*unit, sc)
        sc = nxt


def _attn_call(q, k, v, sinks):
    b, s, d = q.shape
    W = WINDOW
    tq = ATTN_QUERIES
    kw = k.shape[2]
    per = tq // W
    cur = lambda w: pl.BlockSpec((1, tq, w), lambda i, j, *_: (i, j, 0))
    prev = lambda w: pl.BlockSpec((1, W, w), lambda i, j, *_: (i, jnp.maximum(j * per - 1, 0), 0))
    return pl.pallas_call(
        _attn_body,
        grid_spec=pltpu.PrefetchScalarGridSpec(
            num_scalar_prefetch=1,
            grid=(b, s // tq),
            in_specs=[cur(d), prev(kw), cur(kw), prev(kw), cur(kw)],
            out_specs=cur(d),
            scratch_shapes=[pltpu.VMEM((tq + W, kw), BF16), pltpu.VMEM((tq + W, kw), BF16)]),
        out_shape=jax.ShapeDtypeStruct((b, s, d), BF16),
        compiler_params=pltpu.CompilerParams(
            dimension_semantics=("parallel", "parallel"), vmem_limit_bytes=VMEM_LIMIT),
        name="swa_attn",
    )(sinks, q, k, k, v, v)


def kernel(x, p, positions, a_w_in, a_b_igate, a_b_fgate, a_head_norm_g, a_w_out, kv_w, kv_b, b_w_q, b_b_q, b_sinks, b_w_o, b_b_o, mix_ln_g, mix_ln_b, mlp_w_up, mlp_w_down, mlp_ln_g, mlp_ln_b, ple_w_gate, ple_b_gate, ple_w_proj):
    B, S, D = x.shape
    N = B * S
    row = lambda v: v.reshape(1, -1).astype(F32)

    def ffn(i, a, xs, wo, bo):
        return _ffn_call(i, a, xs, p.reshape(DEPTH, N, PLE_DIM), wo.astype(BF16), row(bo),
                         row(mix_ln_g[i]), row(mix_ln_b[i]),
                         mlp_w_up[i].astype(BF16), mlp_w_down[i].astype(BF16),
                         row(mlp_ln_g[i]), row(mlp_ln_b[i]),
                         ple_w_gate[i].astype(BF16), row(ple_b_gate[i]), ple_w_proj[i].astype(BF16))

    qk_w, v_w = M_HEADS * M_DK, M_HEADS * M_DV
    w_in = a_w_in[0]
    wqt = w_in[:, :qk_w].T.astype(BF16)
    wk = w_in[:, qk_w:2 * qk_w].astype(BF16)
    wvt = w_in[:, 2 * qk_w:2 * qk_w + v_w].T.astype(BF16)
    wot = w_in[:, 2 * qk_w + v_w:2 * qk_w + 2 * v_w].T.astype(BF16)
    wg = w_in[:, 2 * qk_w + 2 * v_w:]
    wgr = jnp.pad(wg.T, ((0, 2 * M_HEADS), (0, 0))).astype(BF16)
    bgr = jnp.concatenate([a_b_igate[0], a_b_fgate[0]]).reshape(2 * M_HEADS, 1).astype(F32)
    qt, k, vt, ogt, b_rows, cm_rows, g_cols = _mlstm_proj_call(x, wqt, wk, wvt, wot, wgr, bgr)
    gain_b = jnp.broadcast_to(a_head_norm_g[0].astype(F32)[:, None], (D, LANES))
    hg = _mlstm_call(qt, k, vt, ogt, b_rows, cm_rows, g_cols, gain_b)
    xs = ffn(0, hg.reshape(N, D), x.reshape(N, D), a_w_out[0], jnp.zeros((D,), F32))

    half = ROPE_DIM // 2
    inv_freq = jnp.power(ROPE_THETA, -jnp.arange(half, dtype=F32) * (2.0 / ROPE_DIM))
    invf = jnp.tile(inv_freq, LANES // half).reshape(1, LANES)
    qr, kr, vr = _qkv_call(xs, positions.reshape(N, 1), b_w_q[0].astype(BF16), row(b_b_q[0]),
                           kv_w.astype(BF16), row(kv_b), invf)
    att = _attn_call(qr.reshape(B, S, D), kr.reshape(B, S, -1), vr.reshape(B, S, -1), b_sinks[0].astype(F32))
    xs = ffn(1, att.reshape(N, D), xs, b_w_o[0], b_b_o[0])
    return xs.reshape(B, S, D)
```

```python
import functools

import jax
import jax.numpy as jnp
import numpy as np
from jax import lax
from jax.experimental import pallas as pl
from jax.experimental.pallas import tpu as pltpu

F32 = jnp.float32
BF16 = jnp.bfloat16

D_MODEL = 1024
DEPTH = 2
M_HEADS = 4
M_DV = D_MODEL // M_HEADS
M_DK = M_DV // 2
GATE_CAP = 15.0
A_HEAD_DIM = 64
A_Q_HEADS = D_MODEL // A_HEAD_DIM
A_KV_HEADS = 4
A_GROUP = A_Q_HEADS // A_KV_HEADS
WINDOW = 128
ROPE_DIM = A_HEAD_DIM // 4
ROPE_THETA = 500000.0
D_FF = 4 * D_MODEL
PLE_DIM = 256
LN_EPS = 1e-5
LOG2_E = 1.4426950408889634
DEEPNORM_ALPHA = (2 * DEPTH) ** 0.25

LANES = 128
VMEM_LIMIT = 56 * 1024 * 1024

FFN_TOKENS = 512
FFN_SUBTILES = 2
FF_CHUNK = 1024
PROJ_TOKENS = 512
M_BLOCK = 256
M_STEP_TOKENS = 512
M_AUG = 16
ATTN_QUERIES = 512

_NT = (((1,), (1,)), ((), ()))


def _dot(a, b):
    return jnp.dot(a, b, preferred_element_type=F32)


def _dot_nt(a, b):
    return lax.dot_general(a, b, _NT, preferred_element_type=F32)


def _resident(shape):
    zeros = (0,) * len(shape)
    return pl.BlockSpec(shape, lambda *_: zeros, pipeline_mode=pl.Buffered(1))


def _layer_norm(y, g, b):
    mu = jnp.mean(y, axis=-1, keepdims=True)
    yc = y - mu
    var = jnp.mean(yc * yc, axis=-1, keepdims=True)
    return yc * lax.rsqrt(var + LN_EPS) * g + b


def _ffn_body(a_ref, x_ref, p_ref, wo_ref, bo_ref, g1_ref, b1_ref, wup_ref, wdn_ref,
              g2_ref, b2_ref, wpg_ref, bpg_ref, wpp_ref, o_ref):
    sub = x_ref.shape[0] // FFN_SUBTILES
    rows = [slice(s * sub, (s + 1) * sub) for s in range(FFN_SUBTILES)]

    def mlp(x1):
        x1b = x1.astype(BF16)
        acc = None
        for c in range(D_FF // FF_CHUNK):
            cols = slice(c * FF_CHUNK, (c + 1) * FF_CHUNK)
            h = jnp.maximum(_dot(x1b, wup_ref[:, cols]), 0.0)
            d = _dot((h * h).astype(BF16), wdn_ref[cols, :])
            acc = d if acc is None else acc + d
        return acc

    mix = [_dot(a_ref[r, :], wo_ref[...]) + bo_ref[...] for r in rows]
    x1 = [_layer_norm(DEEPNORM_ALPHA * x_ref[r, :] + m, g1_ref[...], b1_ref[...]) for r, m in zip(rows, mix)]
    acc = [mlp(v) for v in x1]
    for r, v, a in zip(rows, x1, acc):
        x2 = _layer_norm(DEEPNORM_ALPHA * v + a, g2_ref[...], b2_ref[...])
        gate = jax.nn.sigmoid(_dot(x2.astype(BF16), wpg_ref[...]) + bpg_ref[...])
        pe = _dot(p_ref[r, :].astype(BF16), wpp_ref[...])
        o_ref[r, :] = x2 + gate * pe


def _ffn_call(layer, a, x, p, wo, bo, g1, b1, wup, wdn, g2, b2, wpg, bpg, wpp):
    n, d = x.shape
    tm = FFN_TOKENS
    tok = lambda w: pl.BlockSpec((tm, w), lambda i: (i, 0))
    return pl.pallas_call(
        _ffn_body,
        grid=(n // tm,),
        in_specs=[tok(d), tok(d), pl.BlockSpec((None, tm, PLE_DIM), lambda i: (layer, i, 0)),
                  _resident(wo.shape), _resident(bo.shape), _resident(g1.shape), _resident(b1.shape),
                  _resident(wup.shape), _resident(wdn.shape), _resident(g2.shape), _resident(b2.shape),
                  _resident(wpg.shape), _resident(bpg.shape), _resident(wpp.shape)],
        out_specs=tok(d),
        out_shape=jax.ShapeDtypeStruct((n, d), F32),
        compiler_params=pltpu.CompilerParams(
            dimension_semantics=("parallel",), vmem_limit_bytes=VMEM_LIMIT),
        name="ffn",
    )(a, x, p, wo, bo, g1, b1, wup, wdn, g2, b2, wpg, bpg, wpp)


def _log_sigmoid(z):
    return jnp.minimum(z, 0.0) - jnp.log1p(jnp.exp(-jnp.abs(z)))


def _soft_cap(z):
    return GATE_CAP * jnp.tanh(z / GATE_CAP)


def _mlstm_proj_body(x_ref, wqt_ref, wk_ref, wvt_ref, wot_ref, wgr_ref, bgr_ref, gain_ref,
                     qt_ref, k_ref, vt_ref, ogt_ref, b_ref, cm_ref, gc_ref):
    xb = x_ref[0].astype(BF16)
    xtb = x_ref[0].T.astype(BF16)
    t = xb.shape[0]
    L = M_BLOCK
    z = _soft_cap(_dot(wgr_ref[...], xtb)[0:2 * M_HEADS] + bgr_ref[...])
    gain = jnp.concatenate([gain_ref[...]] * (t // LANES), axis=1)
    ogt_ref[0] = (jax.nn.sigmoid(_dot(wot_ref[...], xtb)) * gain).astype(BF16)
    log_i = jnp.concatenate([z[:M_HEADS]] * 2, axis=0) * LOG2_E
    log_f = _log_sigmoid(jnp.concatenate([z[M_HEADS:]] * 2, axis=0)) * LOG2_E
    qt_ref[0] = (_dot(wqt_ref[...], xtb) * (M_DK ** -0.5)).astype(BF16)
    vt_ref[0] = _dot(wvt_ref[...], xtb).astype(BF16)
    pos = lax.broadcasted_iota(jnp.int32, log_f.shape, 1) & (L - 1)

    def block_scan(v, combine, identity):
        shift = 1
        while shift < L:
            v = combine(v, jnp.where(pos >= shift, pltpu.roll(v, shift, 1), identity))
            shift *= 2
        return v

    b = block_scan(log_f, jnp.add, 0.0)
    g = log_i - b
    cm = block_scan(g, jnp.maximum, -jnp.inf)
    b_ref[0] = b
    cm_ref[0] = cm
    gc_ref[0] = jnp.concatenate([g, jnp.zeros((LANES - 2 * M_HEADS, t), F32)], axis=0).T
    k_ref[0] = _dot(xb, wk_ref[...]).astype(BF16)


def _mlstm_proj_call(x, wqt, wk, wvt, wot, wgr, bgr, gain_b):
    b, s, d = x.shape
    t = PROJ_TOKENS
    qk_w = M_HEADS * M_DK
    tok_major = lambda w: pl.BlockSpec((1, t, w), lambda i, j: (i, j, 0))
    feat_major = lambda w: pl.BlockSpec((1, w, t), lambda i, j: (i, 0, j))
    weights = (wqt, wk, wvt, wot, wgr, bgr, gain_b)
    return pl.pallas_call(
        _mlstm_proj_body,
        grid=(b, s // t),
        in_specs=[tok_major(d)] + [_resident(w.shape) for w in weights],
        out_specs=[feat_major(qk_w), tok_major(qk_w), feat_major(d), feat_major(d),
                   feat_major(2 * M_HEADS), feat_major(2 * M_HEADS), tok_major(LANES)],
        out_shape=[jax.ShapeDtypeStruct((b, qk_w, s), BF16),
                   jax.ShapeDtypeStruct((b, s, qk_w), BF16),
                   jax.ShapeDtypeStruct((b, d, s), BF16),
                   jax.ShapeDtypeStruct((b, d, s), BF16),
                   jax.ShapeDtypeStruct((b, 2 * M_HEADS, s), F32),
                   jax.ShapeDtypeStruct((b, 2 * M_HEADS, s), F32),
                   jax.ShapeDtypeStruct((b, s, LANES), F32)],
        compiler_params=pltpu.CompilerParams(
            dimension_semantics=("parallel", "parallel"), vmem_limit_bytes=VMEM_LIMIT),
        name="mlstm_proj",
    )(x, *weights)


def _mlstm_body(qt_ref, k_ref, vt_ref, ogt_ref, b_ref, cm_ref, gc_ref, o_ref, c_ref, m_ref):
    @pl.when(pl.program_id(1) == 0)
    def _():
        c_ref[...] = jnp.zeros_like(c_ref)
        m_ref[...] = jnp.zeros_like(m_ref)

    L = M_BLOCK
    reps = L // LANES
    keep_diag = (lax.broadcasted_iota(jnp.int32, (LANES, LANES), 0)
                 <= lax.broadcasted_iota(jnp.int32, (LANES, LANES), 1))
    ones_rows = jnp.ones((M_AUG, L), BF16)

    early = []
    for blk in range(k_ref.shape[1] // L):
        tok = slice(blk * L, (blk + 1) * L)
        b_rows = b_ref[0, :, tok]
        cm = cm_ref[0, :, tok]
        g_cols = gc_ref[0, tok, :]
        for h in range(M_HEADS):
            qt = qt_ref[0, h * M_DK:(h + 1) * M_DK, tok]
            kh = k_ref[0, tok, h * M_DK:(h + 1) * M_DK]
            lhs = jnp.concatenate([vt_ref[0, h * M_DV:(h + 1) * M_DV, tok], ones_rows], axis=0)
            m_prev = m_ref[h]
            m_row = jnp.maximum(cm[h:h + 1], m_prev)
            m_last = m_row[:, L - 1:L]
            g_b = jnp.broadcast_to(g_cols[:, h:h + 1], (L, LANES))
            c_old = c_ref[h]
            st = _dot(kh, qt)
            cq = _dot(c_old.astype(BF16), qt)
            kw = (kh.astype(F32) * jnp.exp2(g_b - m_last)).astype(BF16)
            c_ref[h] = jnp.exp2(m_prev - m_last) * c_old + _dot(lhs, kw)
            m_ref[h] = b_rows[h:h + 1, L - 1:L] + m_last
            early.append((tok, h, lhs, b_rows[h:h + 1], m_prev, m_row, g_b, st, cq))

    zeros_q = jnp.zeros((LANES, LANES), F32)
    for tok, h, lhs, b_row, m_prev, m_row, g_b, st, cq in early:
        sd_rows = []
        for i in range(reps):
            src_rows = slice(i * LANES, (i + 1) * LANES)
            quads = []
            for j in range(reps):
                tgt_lanes = slice(j * LANES, (j + 1) * LANES)
                if j < i:
                    quads.append(zeros_q)
                    continue
                dq = jnp.exp2(g_b[src_rows] - m_row[:, tgt_lanes])
                if j == i:
                    dq = jnp.where(keep_diag, dq, 0.0)
                quads.append(st[src_rows, tgt_lanes] * dq)
            sd_rows.append(jnp.concatenate(quads, axis=1))
        sd = jnp.concatenate(sd_rows, axis=0).astype(BF16)
        tot = _dot(lhs, sd) + cq * jnp.exp2(m_prev - m_row)
        num, den = tot[:M_DV], tot[M_DV:M_DV + 1]
        dmax = jnp.maximum(jnp.abs(den), jnp.exp2(-(b_row + m_row)))
        mu = jnp.mean(num, axis=0, keepdims=True)
        hc = num - mu
        var = jnp.mean(hc * hc, axis=0, keepdims=True)
        hn = hc * lax.rsqrt(var + LN_EPS * dmax * dmax)
        og = ogt_ref[0, h * M_DV:(h + 1) * M_DV, tok].astype(F32)
        o_ref[0, tok, h * M_DV:(h + 1) * M_DV] = (hn * og).T.astype(BF16)


def _mlstm_call(qt, k, vt, ogt, b_rows, cm_rows, g_cols):
    b, d, s = vt.shape
    L = M_STEP_TOKENS
    qk_w = M_HEADS * M_DK
    tok_major = lambda w: pl.BlockSpec((1, L, w), lambda i, j: (i, j, 0))
    feat_major = lambda w: pl.BlockSpec((1, w, L), lambda i, j: (i, 0, j))
    return pl.pallas_call(
        _mlstm_body,
        grid=(b, s // L),
        in_specs=[feat_major(qk_w), tok_major(qk_w), feat_major(d), feat_major(d),
                  feat_major(2 * M_HEADS), feat_major(2 * M_HEADS), tok_major(LANES)],
        out_specs=tok_major(d),
        out_shape=jax.ShapeDtypeStruct((b, s, d), BF16),
        scratch_shapes=[pltpu.VMEM((M_HEADS, M_DV + M_AUG, M_DK), F32),
                        pltpu.VMEM((M_HEADS, 1, 1), F32)],
        compiler_params=pltpu.CompilerParams(
            dimension_semantics=("parallel", "arbitrary"), vmem_limit_bytes=VMEM_LIMIT),
        name="mlstm",
    )(qt, k, vt, ogt, b_rows, cm_rows, g_cols)


def _rope_selector():
    half = ROPE_DIM // 2
    sel = np.zeros((LANES, 3 * LANES), np.float32)
    for lane in range(LANES):
        dim, freq = lane % A_HEAD_DIM, lane % half
        if dim < ROPE_DIM:
            sel[freq, lane] = 1.0
        else:
            sel[2 * half, lane] = 1.0
        if dim < half:
            sel[half + freq, LANES + lane] = -1.0
        elif dim < ROPE_DIM:
            sel[half + freq, 2 * LANES + lane] = 1.0
    return jnp.asarray(np.concatenate([sel, sel], axis=0), BF16)


def _qkv_body(x_ref, pos_ref, wq_ref, bq_ref, wkv_ref, bkv_ref, invf_ref, sel_ref, q_ref, k_ref, v_ref):
    t = x_ref.shape[0]
    half = ROPE_DIM // 2
    xb = x_ref[...].astype(BF16)
    q = _dot(xb, wq_ref[...]) + bq_ref[...]
    ang = invf_ref[...] * pos_ref[...].astype(F32)
    table = jnp.concatenate([jnp.cos(ang), jnp.sin(ang), jnp.ones_like(ang),
                             jnp.zeros((LANES - 3 * half, t), F32)], axis=0).T
    hi = table.astype(BF16)
    lo = (table - hi.astype(F32)).astype(BF16)
    coef = _dot(jnp.concatenate([hi, lo], axis=1), sel_ref[...])
    c_self, c_next, c_prev = coef[:, :LANES], coef[:, LANES:2 * LANES], coef[:, 2 * LANES:]

    def rope(z):
        return (z * c_self + pltpu.roll(z, LANES - half, 1) * c_next
                + pltpu.roll(z, half, 1) * c_prev)

    kv = _dot(xb, wkv_ref[...]) + bkv_ref[...]
    for c in range(q.shape[1] // LANES):
        lanes = slice(c * LANES, (c + 1) * LANES)
        q_ref[:, lanes] = rope(q[:, lanes]).astype(BF16)
    low = lax.broadcasted_iota(jnp.int32, (t, LANES), 1) < A_HEAD_DIM

    def dup(z, o_ref, c):
        sw = pltpu.roll(z, A_HEAD_DIM, 1)
        o_ref[:, 2 * c * LANES:(2 * c + 1) * LANES] = jnp.where(low, z, sw).astype(BF16)
        o_ref[:, (2 * c + 1) * LANES:(2 * c + 2) * LANES] = jnp.where(low, sw, z).astype(BF16)

    kv_half = kv.shape[1] // 2
    for c in range(kv_half // LANES):
        dup(rope(kv[:, c * LANES:(c + 1) * LANES]), k_ref, c)
        dup(kv[:, kv_half + c * LANES:kv_half + (c + 1) * LANES], v_ref, c)


def _qkv_call(x, pos, wq, bq, wkv, bkv, invf, sel):
    n, d = x.shape
    t = PROJ_TOKENS
    kw = wkv.shape[1]
    tok = lambda w: pl.BlockSpec((t, w), lambda i: (i, 0))
    return pl.pallas_call(
        _qkv_body,
        grid=(n // t,),
        in_specs=[tok(d), pl.BlockSpec((1, t), lambda i: (0, i)), _resident(wq.shape), _resident(bq.shape),
                  _resident(wkv.shape), _resident(bkv.shape), _resident(invf.shape), _resident(sel.shape)],
        out_specs=[tok(d), tok(kw), tok(kw)],
        out_shape=[jax.ShapeDtypeStruct((n, d), BF16),
                   jax.ShapeDtypeStruct((n, kw), BF16),
                   jax.ShapeDtypeStruct((n, kw), BF16)],
        compiler_params=pltpu.CompilerParams(
            dimension_semantics=("parallel",), vmem_limit_bytes=VMEM_LIMIT),
        name="qkv_rope",
    )(x, pos, wq, bq, wkv, bkv, invf, sel)


def _attn_body(sink_ref, q_ref, kp_ref, kc_ref, vp_ref, vc_ref, o_ref, kf_ref, vf_ref):
    W = WINDOW
    tq = q_ref.shape[1]
    blocks = tq // W
    tile = pl.program_id(1)
    kf_ref[0:W] = kp_ref[0]
    kf_ref[W:] = kc_ref[0]
    vf_ref[0:W] = vp_ref[0]
    vf_ref[W:] = vc_ref[0]
    low = lax.broadcasted_iota(jnp.int32, (2 * W, LANES), 1) < A_HEAD_DIM
    keep_low = jnp.where(low, 1.0, 0.0).astype(BF16)
    keep_high = jnp.where(low, 0.0, 1.0).astype(BF16)
    from_prev = (lax.broadcasted_iota(jnp.int32, (W, W), 1)
                 > lax.broadcasted_iota(jnp.int32, (W, W), 0))
    keep_prev = jnp.where(from_prev, 1.0, 0.0).astype(BF16)
    keep_cur = jnp.where(from_prev, 0.0, 1.0).astype(BF16)
    ones_cols = jnp.concatenate([keep_low, keep_high], axis=0)
    low_out = lax.broadcasted_iota(jnp.int32, (W, LANES), 1) < A_HEAD_DIM
    no_prev = jnp.where(tile > 0, 0.0, -jnp.inf)

    def scores(i, kh):
        kband = kf_ref[i * W:(i + 2) * W, kh * LANES:(kh + 1) * LANES]
        kcat = jnp.concatenate([kband * keep_low, kband * keep_high], axis=0)
        base = kh * A_GROUP * A_HEAD_DIM
        q2 = jnp.concatenate([q_ref[0, i * W:(i + 1) * W, base:base + LANES],
                              q_ref[0, i * W:(i + 1) * W, base + LANES:base + 2 * LANES]], axis=0)
        return _dot_nt(q2, kcat)

    def finish(i, kh, sc):
        vband = vf_ref[i * W:(i + 2) * W, kh * LANES:(kh + 1) * LANES]
        vcat = jnp.concatenate([jnp.concatenate([vband * keep_low, vband * keep_high], axis=0),
                                ones_cols], axis=1)
        base = kh * A_GROUP * A_HEAD_DIM
        slabs, sink_terms = [], []
        for sl in range(2):
            parts, terms = [], []
            for par in range(2):
                s_h = sc[sl * W:(sl + 1) * W, par * 2 * W:(par + 1) * 2 * W]
                s_prev = s_h[:, :W] + no_prev if i == 0 else s_h[:, :W]
                c = jnp.where(from_prev, s_prev, s_h[:, W:])
                m = jnp.max(c, axis=1, keepdims=True)
                e = jnp.exp(c - m).astype(BF16)
                parts += [e * keep_prev, e * keep_cur]
                terms.append(jnp.exp(sink_ref[kh * A_GROUP + 2 * sl + par] - m))
            slabs.append(jnp.concatenate(parts, axis=1))
            sink_terms.append(jnp.where(low_out, terms[0], terms[1]))
        r = _dot(jnp.concatenate(slabs, axis=0), vcat)
        for sl in range(2):
            rows = slice(sl * W, (sl + 1) * W)
            out = r[rows, :LANES] * (1.0 / (r[rows, LANES:] + sink_terms[sl]))
            o_ref[0, i * W:(i + 1) * W, base + sl * LANES:base + (sl + 1) * LANES] = out.astype(BF16)

    units = [(i, kh) for i in range(blocks) for kh in range(A_KV_HEADS)]
    sc = scores(*units[0])
    for u, unit in enumerate(units):
        nxt = scores(*units[u + 1]) if u + 1 < len(units) else None
        finish(*unit, sc)
        sc = nxt


def _attn_call(q, k, v, sinks):
    b, s, d = q.shape
    W = WINDOW
    tq = ATTN_QUERIES
    kw = k.shape[2]
    per = tq // W
    cur = lambda w: pl.BlockSpec((1, tq, w), lambda i, j, *_: (i, j, 0))
    prev = lambda w: pl.BlockSpec((1, W, w), lambda i, j, *_: (i, jnp.maximum(j * per - 1, 0), 0))
    return pl.pallas_call(
        _attn_body,
        grid_spec=pltpu.PrefetchScalarGridSpec(
            num_scalar_prefetch=1,
            grid=(b, s // tq),
            in_specs=[cur(d), prev(kw), cur(kw), prev(kw), cur(kw)],
            out_specs=cur(d),
            scratch_shapes=[pltpu.VMEM((tq + W, kw), BF16), pltpu.VMEM((tq + W, kw), BF16)]),
        out_shape=jax.ShapeDtypeStruct((b, s, d), BF16),
        compiler_params=pltpu.CompilerParams(
            dimension_semantics=("parallel", "parallel"), vmem_limit_bytes=VMEM_LIMIT),
        name="swa_attn",
    )(sinks, q, k, k, v, v)


def kernel(x, p, positions, a_w_in, a_b_igate, a_b_fgate, a_head_norm_g, a_w_out, kv_w, kv_b, b_w_q, b_b_q, b_sinks, b_w_o, b_b_o, mix_ln_g, mix_ln_b, mlp_w_up, mlp_w_down, mlp_ln_g, mlp_ln_b, ple_w_gate, ple_b_gate, ple_w_proj):
    B, S, D = x.shape
    N = B * S
    row = lambda v: v.reshape(1, -1).astype(F32)

    def ffn(i, a, xs, wo, bo):
        return _ffn_call(i, a, xs, p.reshape(DEPTH, N, PLE_DIM), wo.astype(BF16), row(bo),
                         row(mix_ln_g[i]), row(mix_ln_b[i]),
                         mlp_w_up[i].astype(BF16), mlp_w_down[i].astype(BF16),
                         row(mlp_ln_g[i]), row(mlp_ln_b[i]),
                         ple_w_gate[i].astype(BF16), row(ple_b_gate[i]), ple_w_proj[i].astype(BF16))

    qk_w, v_w = M_HEADS * M_DK, M_HEADS * M_DV
    w_in = a_w_in[0]
    wqt = w_in[:, :qk_w].T.astype(BF16)
    wk = w_in[:, qk_w:2 * qk_w].astype(BF16)
    wvt = w_in[:, 2 * qk_w:2 * qk_w + v_w].T.astype(BF16)
    wot = w_in[:, 2 * qk_w + v_w:2 * qk_w + 2 * v_w].T.astype(BF16)
    wg = w_in[:, 2 * qk_w + 2 * v_w:]
    wgr = jnp.pad(wg.T, ((0, 2 * M_HEADS), (0, 0))).astype(BF16)
    bgr = jnp.concatenate([a_b_igate[0], a_b_fgate[0]]).reshape(2 * M_HEADS, 1).astype(F32)
    gain_b = jnp.broadcast_to(a_head_norm_g[0].astype(F32)[:, None], (D, LANES))
    qt, k, vt, ogt, b_rows, cm_rows, g_cols = _mlstm_proj_call(x, wqt, wk, wvt, wot, wgr, bgr, gain_b)
    hg = _mlstm_call(qt, k, vt, ogt, b_rows, cm_rows, g_cols)
    xs = ffn(0, hg.reshape(N, D), x.reshape(N, D), a_w_out[0], jnp.zeros((D,), F32))

    half = ROPE_DIM // 2
    inv_freq = jnp.power(ROPE_THETA, -jnp.arange(half, dtype=F32) * (2.0 / ROPE_DIM))
    q_scale = A_HEAD_DIM ** -0.5
    qr, kr, vr = _qkv_call(xs, positions.reshape(1, N), (b_w_q[0] * q_scale).astype(BF16), row(b_b_q[0] * q_scale),
                           kv_w.astype(BF16), row(kv_b), inv_freq.reshape(half, 1), _rope_selector())
    att = _attn_call(qr.reshape(B, S, D), kr.reshape(B, S, -1), vr.reshape(B, S, -1), b_sinks[0].astype(F32))
    xs = ffn(1, att.reshape(N, D), xs, b_w_o[0], b_b_o[0])
    return xs.reshape(B, S, D)
```

```python
import functools

import jax
import jax.numpy as jnp
import numpy as np
from jax import lax
from jax.experimental import pallas as pl
from jax.experimental.pallas import tpu as pltpu

F32 = jnp.float32
BF16 = jnp.bfloat16

D_MODEL = 1024
DEPTH = 2
M_HEADS = 4
M_DV = D_MODEL // M_HEADS
M_DK = M_DV // 2
GATE_CAP = 15.0
A_HEAD_DIM = 64
A_Q_HEADS = D_MODEL // A_HEAD_DIM
A_KV_HEADS = 4
A_GROUP = A_Q_HEADS // A_KV_HEADS
WINDOW = 128
ROPE_DIM = A_HEAD_DIM // 4
ROPE_THETA = 500000.0
D_FF = 4 * D_MODEL
PLE_DIM = 256
LN_EPS = 1e-5
LOG2_E = 1.4426950408889634
DEEPNORM_ALPHA = (2 * DEPTH) ** 0.25

LANES = 128
VMEM_LIMIT = 56 * 1024 * 1024

FFN_TOKENS = 512
FFN_SUBTILES = 2
FF_CHUNK = 1024
PROJ_TOKENS = 512
M_BLOCK = 256
M_STEP_TOKENS = 512
M_AUG = 16
QKV_Q_GROUP = 256
ATTN_QUERIES = 512

_NT = (((1,), (1,)), ((), ()))


def _dot(a, b):
    return jnp.dot(a, b, preferred_element_type=F32)


def _dot_nt(a, b):
    return lax.dot_general(a, b, _NT, preferred_element_type=F32)


def _resident(shape):
    zeros = (0,) * len(shape)
    return pl.BlockSpec(shape, lambda *_: zeros, pipeline_mode=pl.Buffered(1))


def _layer_norm(y, g, b):
    mu = jnp.mean(y, axis=-1, keepdims=True)
    yc = y - mu
    var = jnp.mean(yc * yc, axis=-1, keepdims=True)
    return yc * lax.rsqrt(var + LN_EPS) * g + b


def _ffn_body(a_ref, x_ref, p_ref, wo_ref, bo_ref, g1_ref, b1_ref, wup_ref, wdn_ref,
              g2_ref, b2_ref, wpg_ref, bpg_ref, wpp_ref, o_ref):
    sub = x_ref.shape[0] // FFN_SUBTILES
    rows = [slice(s * sub, (s + 1) * sub) for s in range(FFN_SUBTILES)]

    def mlp(x1):
        x1b = x1.astype(BF16)
        acc = None
        for c in range(D_FF // FF_CHUNK):
            cols = slice(c * FF_CHUNK, (c + 1) * FF_CHUNK)
            h = jnp.maximum(_dot(x1b, wup_ref[:, cols]), 0.0)
            d = _dot((h * h).astype(BF16), wdn_ref[cols, :])
            acc = d if acc is None else acc + d
        return acc

    mix = [_dot(a_ref[r, :], wo_ref[...]) + bo_ref[...] for r in rows]
    x1 = [_layer_norm(DEEPNORM_ALPHA * x_ref[r, :] + m, g1_ref[...], b1_ref[...]) for r, m in zip(rows, mix)]
    acc = [mlp(v) for v in x1]
    for r, v, a in zip(rows, x1, acc):
        x2 = _layer_norm(DEEPNORM_ALPHA * v + a, g2_ref[...], b2_ref[...])
        gate = jax.nn.sigmoid(_dot(x2.astype(BF16), wpg_ref[...]) + bpg_ref[...])
        pe = _dot(p_ref[r, :].astype(BF16), wpp_ref[...])
        o_ref[r, :] = x2 + gate * pe


def _ffn_call(layer, a, x, p, wo, bo, g1, b1, wup, wdn, g2, b2, wpg, bpg, wpp):
    n, d = x.shape
    tm = FFN_TOKENS
    tok = lambda w: pl.BlockSpec((tm, w), lambda i: (i, 0))
    return pl.pallas_call(
        _ffn_body,
        grid=(n // tm,),
        in_specs=[tok(d), tok(d), pl.BlockSpec((None, tm, PLE_DIM), lambda i: (layer, i, 0)),
                  _resident(wo.shape), _resident(bo.shape), _resident(g1.shape), _resident(b1.shape),
                  _resident(wup.shape), _resident(wdn.shape), _resident(g2.shape), _resident(b2.shape),
                  _resident(wpg.shape), _resident(bpg.shape), _resident(wpp.shape)],
        out_specs=tok(d),
        out_shape=jax.ShapeDtypeStruct((n, d), F32),
        compiler_params=pltpu.CompilerParams(
            dimension_semantics=("parallel",), vmem_limit_bytes=VMEM_LIMIT),
        name="ffn",
    )(a, x, p, wo, bo, g1, b1, wup, wdn, g2, b2, wpg, bpg, wpp)


def _log_sigmoid(z):
    return jnp.minimum(z, 0.0) - jnp.log1p(jnp.exp(-jnp.abs(z)))


def _soft_cap(z):
    return GATE_CAP * jnp.tanh(z / GATE_CAP)


def _mlstm_proj_body(x_ref, wqt_ref, wk_ref, wvt_ref, wot_ref, wgr_ref, bgr_ref, gain_ref,
                     qt_ref, k_ref, vt_ref, ogt_ref, b_ref, cm_ref, gc_ref):
    xb = x_ref[0].astype(BF16)
    xtb = x_ref[0].T.astype(BF16)
    t = xb.shape[0]
    L = M_BLOCK
    k_ref[0] = _dot(xb, wk_ref[...]).astype(BF16)
    z = _soft_cap(_dot(wgr_ref[...], xtb)[0:2 * M_HEADS] + bgr_ref[...])
    gain = jnp.concatenate([gain_ref[...]] * (t // LANES), axis=1)
    ogt_ref[0] = (jax.nn.sigmoid(_dot(wot_ref[...], xtb)) * gain).astype(BF16)
    log_i = jnp.concatenate([z[:M_HEADS]] * 2, axis=0) * LOG2_E
    log_f = _log_sigmoid(jnp.concatenate([z[M_HEADS:]] * 2, axis=0)) * LOG2_E
    qt_ref[0] = (_dot(wqt_ref[...], xtb) * (M_DK ** -0.5)).astype(BF16)
    pos = lax.broadcasted_iota(jnp.int32, log_f.shape, 1) & (L - 1)

    def block_scan(v, combine, identity):
        shift = 1
        while shift < L:
            v = combine(v, jnp.where(pos >= shift, pltpu.roll(v, shift, 1), identity))
            shift *= 2
        return v

    b = block_scan(log_f, jnp.add, 0.0)
    g = log_i - b
    cm = block_scan(g, jnp.maximum, -jnp.inf)
    b_ref[0] = b
    cm_ref[0] = cm
    gc_ref[0] = jnp.concatenate([g, jnp.zeros((LANES - 2 * M_HEADS, t), F32)], axis=0).T
    vt_ref[0] = _dot(wvt_ref[...], xtb).astype(BF16)


def _mlstm_proj_call(x, wqt, wk, wvt, wot, wgr, bgr, gain_b):
    b, s, d = x.shape
    t = PROJ_TOKENS
    qk_w = M_HEADS * M_DK
    tok_major = lambda w: pl.BlockSpec((1, t, w), lambda i, j: (i, j, 0))
    feat_major = lambda w: pl.BlockSpec((1, w, t), lambda i, j: (i, 0, j))
    weights = (wqt, wk, wvt, wot, wgr, bgr, gain_b)
    return pl.pallas_call(
        _mlstm_proj_body,
        grid=(b, s // t),
        in_specs=[tok_major(d)] + [_resident(w.shape) for w in weights],
        out_specs=[feat_major(qk_w), tok_major(qk_w), feat_major(d), feat_major(d),
                   feat_major(2 * M_HEADS), feat_major(2 * M_HEADS), tok_major(LANES)],
        out_shape=[jax.ShapeDtypeStruct((b, qk_w, s), BF16),
                   jax.ShapeDtypeStruct((b, s, qk_w), BF16),
                   jax.ShapeDtypeStruct((b, d, s), BF16),
                   jax.ShapeDtypeStruct((b, d, s), BF16),
                   jax.ShapeDtypeStruct((b, 2 * M_HEADS, s), F32),
                   jax.ShapeDtypeStruct((b, 2 * M_HEADS, s), F32),
                   jax.ShapeDtypeStruct((b, s, LANES), F32)],
        compiler_params=pltpu.CompilerParams(
            dimension_semantics=("parallel", "parallel"), vmem_limit_bytes=VMEM_LIMIT),
        name="mlstm_proj",
    )(x, *weights)


def _mlstm_body(qt_ref, k_ref, vt_ref, ogt_ref, b_ref, cm_ref, gc_ref, o_ref, c_ref, m_ref):
    @pl.when(pl.program_id(1) == 0)
    def _():
        c_ref[...] = jnp.zeros_like(c_ref)
        m_ref[...] = jnp.zeros_like(m_ref)

    L = M_BLOCK
    reps = L // LANES
    keep_diag = (lax.broadcasted_iota(jnp.int32, (LANES, LANES), 0)
                 <= lax.broadcasted_iota(jnp.int32, (LANES, LANES), 1))
    ones_rows = jnp.ones((M_AUG, L), BF16)

    early = []
    for blk in range(k_ref.shape[1] // L):
        tok = slice(blk * L, (blk + 1) * L)
        b_rows = b_ref[0, :, tok]
        cm = cm_ref[0, :, tok]
        g_cols = gc_ref[0, tok, :]
        for h in range(M_HEADS):
            qt = qt_ref[0, h * M_DK:(h + 1) * M_DK, tok]
            kh = k_ref[0, tok, h * M_DK:(h + 1) * M_DK]
            lhs = jnp.concatenate([vt_ref[0, h * M_DV:(h + 1) * M_DV, tok], ones_rows], axis=0)
            m_prev = m_ref[h]
            m_row = jnp.maximum(cm[h:h + 1], m_prev)
            m_last = m_row[:, L - 1:L]
            g_b = jnp.broadcast_to(g_cols[:, h:h + 1], (L, LANES))
            c_old = c_ref[h]
            st = _dot(kh, qt)
            cq = _dot(c_old.astype(BF16), qt)
            kw = (kh.astype(F32) * jnp.exp2(g_b - m_last)).astype(BF16)
            c_ref[h] = jnp.exp2(m_prev - m_last) * c_old + _dot(lhs, kw)
            m_ref[h] = b_rows[h:h + 1, L - 1:L] + m_last
            early.append((tok, h, lhs, b_rows[h:h + 1], m_prev, m_row, g_b, st, cq))

    zeros_q = jnp.zeros((LANES, LANES), F32)
    for tok, h, lhs, b_row, m_prev, m_row, g_b, st, cq in early:
        sd_rows = []
        for i in range(reps):
            src_rows = slice(i * LANES, (i + 1) * LANES)
            quads = []
            for j in range(reps):
                tgt_lanes = slice(j * LANES, (j + 1) * LANES)
                if j < i:
                    quads.append(zeros_q)
                    continue
                dq = jnp.exp2(g_b[src_rows] - m_row[:, tgt_lanes])
                if j == i:
                    dq = jnp.where(keep_diag, dq, 0.0)
                quads.append(st[src_rows, tgt_lanes] * dq)
            sd_rows.append(jnp.concatenate(quads, axis=1))
        sd = jnp.concatenate(sd_rows, axis=0).astype(BF16)
        tot = _dot(lhs, sd) + cq * jnp.exp2(m_prev - m_row)
        num, den = tot[:M_DV], tot[M_DV:M_DV + 1]
        dmax = jnp.maximum(jnp.abs(den), jnp.exp2(-(b_row + m_row)))
        mu = jnp.mean(num, axis=0, keepdims=True)
        hc = num - mu
        var = jnp.mean(hc * hc, axis=0, keepdims=True)
        hn = hc * lax.rsqrt(var + LN_EPS * dmax * dmax)
        og = ogt_ref[0, h * M_DV:(h + 1) * M_DV, tok].astype(F32)
        o_ref[0, tok, h * M_DV:(h + 1) * M_DV] = (hn * og).T.astype(BF16)


def _mlstm_call(qt, k, vt, ogt, b_rows, cm_rows, g_cols):
    b, d, s = vt.shape
    L = M_STEP_TOKENS
    qk_w = M_HEADS * M_DK
    tok_major = lambda w: pl.BlockSpec((1, L, w), lambda i, j: (i, j, 0))
    feat_major = lambda w: pl.BlockSpec((1, w, L), lambda i, j: (i, 0, j))
    return pl.pallas_call(
        _mlstm_body,
        grid=(b, s // L),
        in_specs=[feat_major(qk_w), tok_major(qk_w), feat_major(d), feat_major(d),
                  feat_major(2 * M_HEADS), feat_major(2 * M_HEADS), tok_major(LANES)],
        out_specs=tok_major(d),
        out_shape=jax.ShapeDtypeStruct((b, s, d), BF16),
        scratch_shapes=[pltpu.VMEM((M_HEADS, M_DV + M_AUG, M_DK), F32),
                        pltpu.VMEM((M_HEADS, 1, 1), F32)],
        compiler_params=pltpu.CompilerParams(
            dimension_semantics=("parallel", "arbitrary"), vmem_limit_bytes=VMEM_LIMIT),
        name="mlstm",
    )(qt, k, vt, ogt, b_rows, cm_rows, g_cols)


def _paired_head_order():
    return [(2 * pair + e) * A_GROUP + g
            for pair in range(A_KV_HEADS // 2) for g in range(A_GROUP) for e in range(2)]


def _rope_selector():
    half = ROPE_DIM // 2
    sel = np.zeros((LANES, 3 * LANES), np.float32)
    for lane in range(LANES):
        dim, freq = lane % A_HEAD_DIM, lane % half
        if dim < ROPE_DIM:
            sel[freq, lane] = 1.0
        else:
            sel[2 * half, lane] = 1.0
        if dim < half:
            sel[half + freq, LANES + lane] = -1.0
        elif dim < ROPE_DIM:
            sel[half + freq, 2 * LANES + lane] = 1.0
    return jnp.asarray(np.concatenate([sel, sel], axis=0), BF16)


def _qkv_body(x_ref, pos_ref, wq_ref, bq_ref, wkv_ref, bkv_ref, invf_ref, sel_ref, q_ref, k_ref, v_ref):
    t = x_ref.shape[0]
    half = ROPE_DIM // 2
    xb = x_ref[...].astype(BF16)
    kv = _dot(xb, wkv_ref[...]) + bkv_ref[...]
    ang = invf_ref[...] * pos_ref[...].astype(F32)
    table = jnp.concatenate([jnp.cos(ang), jnp.sin(ang), jnp.ones_like(ang),
                             jnp.zeros((LANES - 3 * half, t), F32)], axis=0).T
    hi = table.astype(BF16)
    lo = (table - hi.astype(F32)).astype(BF16)
    coef = _dot(jnp.concatenate([hi, lo], axis=1), sel_ref[...])
    c_self, c_next, c_prev = coef[:, :LANES], coef[:, LANES:2 * LANES], coef[:, 2 * LANES:]

    def rope(z):
        return (z * c_self + pltpu.roll(z, LANES - half, 1) * c_next
                + pltpu.roll(z, half, 1) * c_prev)

    group = QKV_Q_GROUP
    q0 = _dot(xb, wq_ref[:, :group]) + bq_ref[:, :group]
    kv_half = kv.shape[1] // 2
    for c in range(kv_half // LANES):
        k_ref[:, c * LANES:(c + 1) * LANES] = rope(kv[:, c * LANES:(c + 1) * LANES]).astype(BF16)
    v_ref[...] = kv[:, kv_half:].astype(BF16)
    for g in range(wq_ref.shape[1] // group):
        nxt = None
        if (g + 1) * group < wq_ref.shape[1]:
            cols = slice((g + 1) * group, (g + 2) * group)
            nxt = _dot(xb, wq_ref[:, cols]) + bq_ref[:, cols]
        for c in range(group // LANES):
            q_ref[:, g * group + c * LANES:g * group + (c + 1) * LANES] = rope(
                q0[:, c * LANES:(c + 1) * LANES]).astype(BF16)
        q0 = nxt


def _qkv_call(x, pos, wq, bq, wkv, bkv, invf, sel):
    n, d = x.shape
    t = PROJ_TOKENS
    kw = wkv.shape[1] // 2
    tok = lambda w: pl.BlockSpec((t, w), lambda i: (i, 0))
    return pl.pallas_call(
        _qkv_body,
        grid=(n // t,),
        in_specs=[tok(d), pl.BlockSpec((1, t), lambda i: (0, i)), _resident(wq.shape), _resident(bq.shape),
                  _resident(wkv.shape), _resident(bkv.shape), _resident(invf.shape), _resident(sel.shape)],
        out_specs=[tok(d), tok(kw), tok(kw)],
        out_shape=[jax.ShapeDtypeStruct((n, d), BF16),
                   jax.ShapeDtypeStruct((n, kw), BF16),
                   jax.ShapeDtypeStruct((n, kw), BF16)],
        compiler_params=pltpu.CompilerParams(
            dimension_semantics=("parallel",), vmem_limit_bytes=VMEM_LIMIT),
        name="qkv_rope",
    )(x, pos, wq, bq, wkv, bkv, invf, sel)


def _attn_body(sink_ref, q_ref, kp_ref, kc_ref, vp_ref, vc_ref, o_ref, kf_ref, vf_ref):
    W = WINDOW
    tq = q_ref.shape[1]
    blocks = tq // W
    tile = pl.program_id(1)
    kf_ref[0:W] = kp_ref[0]
    kf_ref[W:] = kc_ref[0]
    vf_ref[0:W] = vp_ref[0]
    vf_ref[W:] = vc_ref[0]
    low = lax.broadcasted_iota(jnp.int32, (2 * W, LANES), 1) < A_HEAD_DIM
    keep_low = jnp.where(low, 1.0, 0.0).astype(BF16)
    keep_high = jnp.where(low, 0.0, 1.0).astype(BF16)
    from_prev = (lax.broadcasted_iota(jnp.int32, (W, W), 1)
                 > lax.broadcasted_iota(jnp.int32, (W, W), 0))
    keep_prev = jnp.where(from_prev, 1.0, 0.0).astype(BF16)
    keep_cur = jnp.where(from_prev, 0.0, 1.0).astype(BF16)
    ones_cols = jnp.concatenate([keep_low, keep_high], axis=0)
    low_out = lax.broadcasted_iota(jnp.int32, (W, LANES), 1) < A_HEAD_DIM
    no_prev = jnp.where(tile > 0, 0.0, -jnp.inf)

    def scores(i, pair):
        kband = kf_ref[i * W:(i + 2) * W, pair * LANES:(pair + 1) * LANES]
        kcat = jnp.concatenate([kband * keep_low, kband * keep_high], axis=0)
        base = pair * A_GROUP * LANES
        qs = jnp.concatenate([q_ref[0, i * W:(i + 1) * W, base + g * LANES:base + (g + 1) * LANES]
                              for g in range(A_GROUP)], axis=0)
        return _dot_nt(qs, kcat)

    def finish(i, pair, sc):
        vband = vf_ref[i * W:(i + 2) * W, pair * LANES:(pair + 1) * LANES]
        vcat = jnp.concatenate([jnp.concatenate([vband * keep_low, vband * keep_high], axis=0),
                                ones_cols], axis=1)
        base = pair * A_GROUP * LANES
        slabs, sink_terms = [], []
        for g in range(A_GROUP):
            parts, terms = [], []
            for par in range(2):
                s_h = sc[g * W:(g + 1) * W, par * 2 * W:(par + 1) * 2 * W]
                s_prev = s_h[:, :W] + no_prev if i == 0 else s_h[:, :W]
                c = jnp.where(from_prev, s_prev, s_h[:, W:])
                m = jnp.max(c, axis=1, keepdims=True)
                e = jnp.exp(c - m).astype(BF16)
                parts += [e * keep_prev, e * keep_cur]
                terms.append(jnp.exp(sink_ref[(2 * pair + par) * A_GROUP + g] - m))
            slabs.append(jnp.concatenate(parts, axis=1))
            sink_terms.append(jnp.where(low_out, terms[0], terms[1]))
        r = _dot(jnp.concatenate(slabs, axis=0), vcat)
        for g in range(A_GROUP):
            rows = slice(g * W, (g + 1) * W)
            out = r[rows, :LANES] * (1.0 / (r[rows, LANES:] + sink_terms[g]))
            o_ref[0, i * W:(i + 1) * W, base + g * LANES:base + (g + 1) * LANES] = out.astype(BF16)

    units = [(i, pair) for i in range(blocks) for pair in range(A_KV_HEADS // 2)]
    sc = scores(*units[0])
    for u, unit in enumerate(units):
        nxt = scores(*units[u + 1]) if u + 1 < len(units) else None
        finish(*unit, sc)
        sc = nxt


def _attn_call(q, k, v, sinks):
    b, s, d = q.shape
    W = WINDOW
    tq = ATTN_QUERIES
    kw = k.shape[2]
    per = tq // W
    cur = lambda w: pl.BlockSpec((1, tq, w), lambda i, j, *_: (i, j, 0))
    prev = lambda w: pl.BlockSpec((1, W, w), lambda i, j, *_: (i, jnp.maximum(j * per - 1, 0), 0))
    return pl.pallas_call(
        _attn_body,
        grid_spec=pltpu.PrefetchScalarGridSpec(
            num_scalar_prefetch=1,
            grid=(b, s // tq),
            in_specs=[cur(d), prev(kw), cur(kw), prev(kw), cur(kw)],
            out_specs=cur(d),
            scratch_shapes=[pltpu.VMEM((tq + W, kw), BF16), pltpu.VMEM((tq + W, kw), BF16)]),
        out_shape=jax.ShapeDtypeStruct((b, s, d), BF16),
        compiler_params=pltpu.CompilerParams(
            dimension_semantics=("parallel", "parallel"), vmem_limit_bytes=VMEM_LIMIT),
        name="swa_attn",
    )(sinks, q, k, k, v, v)


def kernel(x, p, positions, a_w_in, a_b_igate, a_b_fgate, a_head_norm_g, a_w_out, kv_w, kv_b, b_w_q, b_b_q, b_sinks, b_w_o, b_b_o, mix_ln_g, mix_ln_b, mlp_w_up, mlp_w_down, mlp_ln_g, mlp_ln_b, ple_w_gate, ple_b_gate, ple_w_proj):
    B, S, D = x.shape
    N = B * S
    row = lambda v: v.reshape(1, -1).astype(F32)

    def ffn(i, a, xs, wo, bo):
        return _ffn_call(i, a, xs, p.reshape(DEPTH, N, PLE_DIM), wo.astype(BF16), row(bo),
                         row(mix_ln_g[i]), row(mix_ln_b[i]),
                         mlp_w_up[i].astype(BF16), mlp_w_down[i].astype(BF16),
                         row(mlp_ln_g[i]), row(mlp_ln_b[i]),
                         ple_w_gate[i].astype(BF16), row(ple_b_gate[i]), ple_w_proj[i].astype(BF16))

    qk_w, v_w = M_HEADS * M_DK, M_HEADS * M_DV
    w_in = a_w_in[0]
    wqt = w_in[:, :qk_w].T.astype(BF16)
    wk = w_in[:, qk_w:2 * qk_w].astype(BF16)
    wvt = w_in[:, 2 * qk_w:2 * qk_w + v_w].T.astype(BF16)
    wot = w_in[:, 2 * qk_w + v_w:2 * qk_w + 2 * v_w].T.astype(BF16)
    wg = w_in[:, 2 * qk_w + 2 * v_w:]
    wgr = jnp.pad(wg.T, ((0, 2 * M_HEADS), (0, 0))).astype(BF16)
    bgr = jnp.concatenate([a_b_igate[0], a_b_fgate[0]]).reshape(2 * M_HEADS, 1).astype(F32)
    gain_b = jnp.broadcast_to(a_head_norm_g[0].astype(F32)[:, None], (D, LANES))
    qt, k, vt, ogt, b_rows, cm_rows, g_cols = _mlstm_proj_call(x, wqt, wk, wvt, wot, wgr, bgr, gain_b)
    hg = _mlstm_call(qt, k, vt, ogt, b_rows, cm_rows, g_cols)
    xs = ffn(0, hg.reshape(N, D), x.reshape(N, D), a_w_out[0], jnp.zeros((D,), F32))

    half = ROPE_DIM // 2
    inv_freq = jnp.power(ROPE_THETA, -jnp.arange(half, dtype=F32) * (2.0 / ROPE_DIM))
    q_scale = A_HEAD_DIM ** -0.5
    order = jnp.asarray(_paired_head_order())
    by_head = lambda w, axis: jnp.take(
        w.reshape(*w.shape[:axis], A_Q_HEADS, A_HEAD_DIM, *w.shape[axis + 1:]), order, axis=axis
    ).reshape(w.shape)
    wq = by_head(b_w_q[0] * q_scale, 1).astype(BF16)
    bq = row(by_head(b_b_q[0] * q_scale, 0))
    qr, kr, vr = _qkv_call(xs, positions.reshape(1, N), wq, bq, kv_w.astype(BF16), row(kv_b),
                           inv_freq.reshape(half, 1), _rope_selector())
    att = _attn_call(qr.reshape(B, S, D), kr.reshape(B, S, -1), vr.reshape(B, S, -1), b_sinks[0].astype(F32))
    xs = ffn(1, att.reshape(N, D), xs, by_head(b_w_o[0], 0), b_b_o[0])
    return xs.reshape(B, S, D)
```

```python
import functools

import jax
import jax.numpy as jnp
import numpy as np
from jax import lax
from jax.experimental import pallas as pl
from jax.experimental.pallas import tpu as pltpu

F32 = jnp.float32
BF16 = jnp.bfloat16

D_MODEL = 1024
DEPTH = 2
M_HEADS = 4
M_DV = D_MODEL // M_HEADS
M_DK = M_DV // 2
GATE_CAP = 15.0
A_HEAD_DIM = 64
A_Q_HEADS = D_MODEL // A_HEAD_DIM
A_KV_HEADS = 4
A_GROUP = A_Q_HEADS // A_KV_HEADS
WINDOW = 128
ROPE_DIM = A_HEAD_DIM // 4
ROPE_THETA = 500000.0
D_FF = 4 * D_MODEL
PLE_DIM = 256
LN_EPS = 1e-5
LOG2_E = 1.4426950408889634
DEEPNORM_ALPHA = (2 * DEPTH) ** 0.25

LANES = 128
VMEM_LIMIT = 56 * 1024 * 1024

FFN_TOKENS = 512
FFN_SUBTILES = 2
FF_CHUNK = 1024
PROJ_TOKENS = 512
M_BLOCK = 256
M_STEP_TOKENS = 512
M_AUG = 16
QKV_Q_GROUP = 256
ATTN_QUERIES = 512

_NT = (((1,), (1,)), ((), ()))


def _dot(a, b):
    return jnp.dot(a, b, preferred_element_type=F32)


def _dot_nt(a, b):
    return lax.dot_general(a, b, _NT, preferred_element_type=F32)


def _resident(shape):
    zeros = (0,) * len(shape)
    return pl.BlockSpec(shape, lambda *_: zeros, pipeline_mode=pl.Buffered(1))


def _resident_part(block_shape, block_index):
    return pl.BlockSpec(block_shape, lambda *_: block_index, pipeline_mode=pl.Buffered(1))


def _layer_norm(y, g, b):
    mu = jnp.mean(y, axis=-1, keepdims=True)
    yc = y - mu
    var = jnp.mean(yc * yc, axis=-1, keepdims=True)
    return yc * lax.rsqrt(var + LN_EPS) * g + b


def _ffn_body(a_ref, x_ref, p_ref, wo_ref, bo_ref, g1_ref, b1_ref, wup_ref, wdn_ref,
              g2_ref, b2_ref, wpg_ref, bpg_ref, wpp_ref, o_ref):
    sub = x_ref.shape[0] // FFN_SUBTILES
    rows = [slice(s * sub, (s + 1) * sub) for s in range(FFN_SUBTILES)]

    def mlp(x1):
        x1b = x1.astype(BF16)
        acc = None
        for c in range(D_FF // FF_CHUNK):
            cols = slice(c * FF_CHUNK, (c + 1) * FF_CHUNK)
            h = jnp.maximum(_dot(x1b, wup_ref[:, cols]), 0.0)
            d = _dot((h * h).astype(BF16), wdn_ref[cols, :])
            acc = d if acc is None else acc + d
        return acc

    mix = [_dot(a_ref[r, :], wo_ref[...]) + bo_ref[...] for r in rows]
    x1 = [_layer_norm(DEEPNORM_ALPHA * x_ref[r, :] + m, g1_ref[...], b1_ref[...]) for r, m in zip(rows, mix)]
    acc = [mlp(v) for v in x1]
    for r, v, a in zip(rows, x1, acc):
        x2 = _layer_norm(DEEPNORM_ALPHA * v + a, g2_ref[...], b2_ref[...])
        gate = jax.nn.sigmoid(_dot(x2.astype(BF16), wpg_ref[...]) + bpg_ref[...])
        pe = _dot(p_ref[r, :].astype(BF16), wpp_ref[...])
        o_ref[r, :] = x2 + gate * pe


def _ffn_call(layer, a, x, p, wo, bo, g1, b1, wup, wdn, g2, b2, wpg, bpg, wpp):
    n, d = x.shape
    tm = FFN_TOKENS
    tok = lambda w: pl.BlockSpec((tm, w), lambda i: (i, 0))
    of_layer = lambda w: _resident_part((None,) + w.shape[1:], (layer,) + (0,) * (w.ndim - 1))
    return pl.pallas_call(
        _ffn_body,
        grid=(n // tm,),
        in_specs=[tok(d), tok(d), pl.BlockSpec((None, tm, PLE_DIM), lambda i: (layer, i, 0)),
                  _resident(wo.shape), _resident(bo.shape), of_layer(g1), of_layer(b1),
                  of_layer(wup), of_layer(wdn), of_layer(g2), of_layer(b2),
                  of_layer(wpg), of_layer(bpg), of_layer(wpp)],
        out_specs=tok(d),
        out_shape=jax.ShapeDtypeStruct((n, d), F32),
        compiler_params=pltpu.CompilerParams(
            dimension_semantics=("parallel",), vmem_limit_bytes=VMEM_LIMIT),
        name="ffn",
    )(a, x, p, wo, bo, g1, b1, wup, wdn, g2, b2, wpg, bpg, wpp)


def _log_sigmoid(z):
    return jnp.minimum(z, 0.0) - jnp.log1p(jnp.exp(-jnp.abs(z)))


def _soft_cap(z):
    return GATE_CAP * jnp.tanh(z / GATE_CAP)


def _mlstm_proj_body(x_ref, wqt_ref, wkt_ref, wvt_ref, wot_ref, wgr_ref, bgr_ref, gain_ref,
                     qt_ref, k_ref, vt_ref, ogt_ref, b_ref, cm_ref, gc_ref):
    xb = x_ref[0].astype(BF16)
    xtb = x_ref[0].T.astype(BF16)
    t = xb.shape[0]
    L = M_BLOCK
    k_ref[0] = _dot_nt(xb, wkt_ref[...]).astype(BF16)
    z = _soft_cap(_dot(wgr_ref[...], xtb)[0:2 * M_HEADS] + bgr_ref[...])
    gain = jnp.concatenate([gain_ref[...]] * (t // LANES), axis=1)
    ogt_ref[0] = (jax.nn.sigmoid(_dot(wot_ref[...], xtb)) * gain).astype(BF16)
    log_i = jnp.concatenate([z[:M_HEADS]] * 2, axis=0) * LOG2_E
    log_f = _log_sigmoid(jnp.concatenate([z[M_HEADS:]] * 2, axis=0)) * LOG2_E
    qt_ref[0] = (_dot(wqt_ref[...], xtb) * (M_DK ** -0.5)).astype(BF16)
    pos = lax.broadcasted_iota(jnp.int32, log_f.shape, 1) & (L - 1)

    def block_scan(v, combine, identity):
        shift = 1
        while shift < L:
            v = combine(v, jnp.where(pos >= shift, pltpu.roll(v, shift, 1), identity))
            shift *= 2
        return v

    b = block_scan(log_f, jnp.add, 0.0)
    g = log_i - b
    cm = block_scan(g, jnp.maximum, -jnp.inf)
    b_ref[0] = b
    cm_ref[0] = cm
    gc_ref[0] = jnp.concatenate([g, jnp.zeros((LANES - 2 * M_HEADS, t), F32)], axis=0).T
    vt_ref[0] = _dot(wvt_ref[...], xtb).astype(BF16)


def _mlstm_proj_call(x, w_t, bgr, gain_b):
    b, s, d = x.shape
    t = PROJ_TOKENS
    qk_w = M_HEADS * M_DK
    tok_major = lambda w: pl.BlockSpec((1, t, w), lambda i, j: (i, j, 0))
    feat_major = lambda w: pl.BlockSpec((1, w, t), lambda i, j: (i, 0, j))
    rows_of = lambda start, size: _resident_part((size, d), (start // size, 0))
    return pl.pallas_call(
        _mlstm_proj_body,
        grid=(b, s // t),
        in_specs=[tok_major(d), rows_of(0, qk_w), rows_of(qk_w, qk_w), rows_of(2 * qk_w, d),
                  rows_of(2 * qk_w + d, d), rows_of(2 * qk_w + 2 * d, M_AUG),
                  _resident(bgr.shape), _resident(gain_b.shape)],
        out_specs=[feat_major(qk_w), tok_major(qk_w), feat_major(d), feat_major(d),
                   feat_major(2 * M_HEADS), feat_major(2 * M_HEADS), tok_major(LANES)],
        out_shape=[jax.ShapeDtypeStruct((b, qk_w, s), BF16),
                   jax.ShapeDtypeStruct((b, s, qk_w), BF16),
                   jax.ShapeDtypeStruct((b, d, s), BF16),
                   jax.ShapeDtypeStruct((b, d, s), BF16),
                   jax.ShapeDtypeStruct((b, 2 * M_HEADS, s), F32),
                   jax.ShapeDtypeStruct((b, 2 * M_HEADS, s), F32),
                   jax.ShapeDtypeStruct((b, s, LANES), F32)],
        compiler_params=pltpu.CompilerParams(
            dimension_semantics=("parallel", "parallel"), vmem_limit_bytes=VMEM_LIMIT),
        name="mlstm_proj",
    )(x, w_t, w_t, w_t, w_t, w_t, bgr, gain_b)


def _mlstm_body(qt_ref, k_ref, vt_ref, ogt_ref, b_ref, cm_ref, gc_ref, o_ref, c_ref, m_ref):
    @pl.when(pl.program_id(1) == 0)
    def _():
        c_ref[...] = jnp.zeros_like(c_ref)
        m_ref[...] = jnp.zeros_like(m_ref)

    L = M_BLOCK
    reps = L // LANES
    keep_diag = (lax.broadcasted_iota(jnp.int32, (LANES, LANES), 0)
                 <= lax.broadcasted_iota(jnp.int32, (LANES, LANES), 1))
    ones_rows = jnp.ones((M_AUG, L), BF16)

    early = []
    for blk in range(k_ref.shape[1] // L):
        tok = slice(blk * L, (blk + 1) * L)
        b_rows = b_ref[0, :, tok]
        cm = cm_ref[0, :, tok]
        g_cols = gc_ref[0, tok, :]
        for h in range(M_HEADS):
            qt = qt_ref[0, h * M_DK:(h + 1) * M_DK, tok]
            kh = k_ref[0, tok, h * M_DK:(h + 1) * M_DK]
            lhs = jnp.concatenate([vt_ref[0, h * M_DV:(h + 1) * M_DV, tok], ones_rows], axis=0)
            m_prev = m_ref[h]
            m_row = jnp.maximum(cm[h:h + 1], m_prev)
            m_last = m_row[:, L - 1:L]
            g_b = jnp.broadcast_to(g_cols[:, h:h + 1], (L, LANES))
            c_old = c_ref[h]
            st = _dot(kh, qt)
            cq = _dot(c_old.astype(BF16), qt)
            kw = (kh.astype(F32) * jnp.exp2(g_b - m_last)).astype(BF16)
            c_ref[h] = jnp.exp2(m_prev - m_last) * c_old + _dot(lhs, kw)
            m_ref[h] = b_rows[h:h + 1, L - 1:L] + m_last
            early.append((tok, h, lhs, b_rows[h:h + 1], m_prev, m_row, g_b, st, cq))

    zeros_q = jnp.zeros((LANES, LANES), F32)
    for tok, h, lhs, b_row, m_prev, m_row, g_b, st, cq in early:
        sd_rows = []
        for i in range(reps):
            src_rows = slice(i * LANES, (i + 1) * LANES)
            quads = []
            for j in range(reps):
                tgt_lanes = slice(j * LANES, (j + 1) * LANES)
                if j < i:
                    quads.append(zeros_q)
                    continue
                dq = jnp.exp2(g_b[src_rows] - m_row[:, tgt_lanes])
                if j == i:
                    dq = jnp.where(keep_diag, dq, 0.0)
                quads.append(st[src_rows, tgt_lanes] * dq)
            sd_rows.append(jnp.concatenate(quads, axis=1))
        sd = jnp.concatenate(sd_rows, axis=0).astype(BF16)
        tot = _dot(lhs, sd) + cq * jnp.exp2(m_prev - m_row)
        num, den = tot[:M_DV], tot[M_DV:M_DV + 1]
        dmax = jnp.maximum(jnp.abs(den), jnp.exp2(-(b_row + m_row)))
        mu = jnp.mean(num, axis=0, keepdims=True)
        hc = num - mu
        var = jnp.mean(hc * hc, axis=0, keepdims=True)
        hn = hc * lax.rsqrt(var + LN_EPS * dmax * dmax)
        og = ogt_ref[0, h * M_DV:(h + 1) * M_DV, tok].astype(F32)
        o_ref[0, tok, h * M_DV:(h + 1) * M_DV] = (hn * og).T.astype(BF16)


def _mlstm_call(qt, k, vt, ogt, b_rows, cm_rows, g_cols):
    b, d, s = vt.shape
    L = M_STEP_TOKENS
    qk_w = M_HEADS * M_DK
    tok_major = lambda w: pl.BlockSpec((1, L, w), lambda i, j: (i, j, 0))
    feat_major = lambda w: pl.BlockSpec((1, w, L), lambda i, j: (i, 0, j))
    return pl.pallas_call(
        _mlstm_body,
        grid=(b, s // L),
        in_specs=[feat_major(qk_w), tok_major(qk_w), feat_major(d), feat_major(d),
                  feat_major(2 * M_HEADS), feat_major(2 * M_HEADS), tok_major(LANES)],
        out_specs=tok_major(d),
        out_shape=jax.ShapeDtypeStruct((b, s, d), BF16),
        scratch_shapes=[pltpu.VMEM((M_HEADS, M_DV + M_AUG, M_DK), F32),
                        pltpu.VMEM((M_HEADS, 1, 1), F32)],
        compiler_params=pltpu.CompilerParams(
            dimension_semantics=("parallel", "arbitrary"), vmem_limit_bytes=VMEM_LIMIT),
        name="mlstm",
    )(qt, k, vt, ogt, b_rows, cm_rows, g_cols)


def _rope_selector():
    half = ROPE_DIM // 2
    sel = np.zeros((LANES, 3 * LANES), np.float32)
    for lane in range(LANES):
        dim, freq = lane % A_HEAD_DIM, lane % half
        if dim < ROPE_DIM:
            sel[freq, lane] = 1.0
        else:
            sel[2 * half, lane] = 1.0
        if dim < half:
            sel[half + freq, LANES + lane] = -1.0
        elif dim < ROPE_DIM:
            sel[half + freq, 2 * LANES + lane] = 1.0
    return jnp.asarray(np.concatenate([sel, sel], axis=0), BF16)


def _qkv_body(x_ref, pos_ref, wq_ref, bq_ref, wkv_ref, bkv_ref, invf_ref, sel_ref, q_ref, k_ref, v_ref):
    t = x_ref.shape[0]
    half = ROPE_DIM // 2
    xb = x_ref[...].astype(BF16)
    kv = _dot(xb, wkv_ref[...]) + bkv_ref[...]
    ang = invf_ref[...] * pos_ref[...].astype(F32)
    table = jnp.concatenate([jnp.cos(ang), jnp.sin(ang), jnp.ones_like(ang),
                             jnp.zeros((LANES - 3 * half, t), F32)], axis=0).T
    hi = table.astype(BF16)
    lo = (table - hi.astype(F32)).astype(BF16)
    coef = _dot(jnp.concatenate([hi, lo], axis=1), sel_ref[...])
    c_self, c_next, c_prev = coef[:, :LANES], coef[:, LANES:2 * LANES], coef[:, 2 * LANES:]

    def rope(z):
        return (z * c_self + pltpu.roll(z, LANES - half, 1) * c_next
                + pltpu.roll(z, half, 1) * c_prev)

    group = QKV_Q_GROUP
    q0 = _dot(xb, wq_ref[:, :group]) + bq_ref[:, :group]
    kv_half = kv.shape[1] // 2
    for c in range(kv_half // LANES):
        k_ref[:, c * LANES:(c + 1) * LANES] = rope(kv[:, c * LANES:(c + 1) * LANES]).astype(BF16)
    v_ref[...] = kv[:, kv_half:].astype(BF16)
    for g in range(wq_ref.shape[1] // group):
        nxt = None
        if (g + 1) * group < wq_ref.shape[1]:
            cols = slice((g + 1) * group, (g + 2) * group)
            nxt = _dot(xb, wq_ref[:, cols]) + bq_ref[:, cols]
        for c in range(group // LANES):
            q_ref[:, g * group + c * LANES:g * group + (c + 1) * LANES] = rope(
                q0[:, c * LANES:(c + 1) * LANES]).astype(BF16)
        q0 = nxt


def _qkv_call(x, pos, wq, bq, wkv, bkv, invf, sel):
    n, d = x.shape
    t = PROJ_TOKENS
    kw = wkv.shape[1] // 2
    tok = lambda w: pl.BlockSpec((t, w), lambda i: (i, 0))
    return pl.pallas_call(
        _qkv_body,
        grid=(n // t,),
        in_specs=[tok(d), pl.BlockSpec((1, t), lambda i: (0, i)), _resident(wq.shape), _resident(bq.shape),
                  _resident(wkv.shape), _resident(bkv.shape), _resident(invf.shape), _resident(sel.shape)],
        out_specs=[tok(d), tok(kw), tok(kw)],
        out_shape=[jax.ShapeDtypeStruct((n, d), BF16),
                   jax.ShapeDtypeStruct((n, kw), BF16),
                   jax.ShapeDtypeStruct((n, kw), BF16)],
        compiler_params=pltpu.CompilerParams(
            dimension_semantics=("parallel",), vmem_limit_bytes=VMEM_LIMIT),
        name="qkv_rope",
    )(x, pos, wq, bq, wkv, bkv, invf, sel)


def _attn_body(sink_ref, q_ref, kp_ref, kc_ref, vp_ref, vc_ref, o_ref, kf_ref, vf_ref):
    W = WINDOW
    tq = q_ref.shape[1]
    blocks = tq // W
    tile = pl.program_id(1)
    kf_ref[0:W] = kp_ref[0]
    kf_ref[W:] = kc_ref[0]
    vf_ref[0:W] = vp_ref[0]
    vf_ref[W:] = vc_ref[0]
    low = lax.broadcasted_iota(jnp.int32, (2 * W, LANES), 1) < A_HEAD_DIM
    keep_low = jnp.where(low, 1.0, 0.0).astype(BF16)
    keep_high = jnp.where(low, 0.0, 1.0).astype(BF16)
    from_prev = (lax.broadcasted_iota(jnp.int32, (W, W), 1)
                 > lax.broadcasted_iota(jnp.int32, (W, W), 0))
    keep_prev = jnp.where(from_prev, 1.0, 0.0).astype(BF16)
    keep_cur = jnp.where(from_prev, 0.0, 1.0).astype(BF16)
    ones_cols = jnp.concatenate([keep_low, keep_high], axis=0)
    low_out = lax.broadcasted_iota(jnp.int32, (W, LANES), 1) < A_HEAD_DIM
    no_prev = jnp.where(tile > 0, 0.0, -jnp.inf)

    def scores(i, pair):
        kband = kf_ref[i * W:(i + 2) * W, pair * LANES:(pair + 1) * LANES]
        kcat = jnp.concatenate([kband * keep_low, kband * keep_high], axis=0)
        base = pair * A_GROUP * LANES
        qs = jnp.concatenate([q_ref[0, i * W:(i + 1) * W, base + g * LANES:base + (g + 1) * LANES]
                              for g in range(A_GROUP)], axis=0)
        return _dot_nt(qs, kcat)

    def finish(i, pair, sc):
        vband = vf_ref[i * W:(i + 2) * W, pair * LANES:(pair + 1) * LANES]
        vcat = jnp.concatenate([jnp.concatenate([vband * keep_low, vband * keep_high], axis=0),
                                ones_cols], axis=1)
        base = pair * A_GROUP * LANES
        slabs, sink_terms = [], []
        for g in range(A_GROUP):
            parts, terms = [], []
            for par in range(2):
                s_h = sc[g * W:(g + 1) * W, par * 2 * W:(par + 1) * 2 * W]
                s_prev = s_h[:, :W] + no_prev if i == 0 else s_h[:, :W]
                c = jnp.where(from_prev, s_prev, s_h[:, W:])
                m = jnp.max(c, axis=1, keepdims=True)
                e = jnp.exp(c - m).astype(BF16)
                parts += [e * keep_prev, e * keep_cur]
                terms.append(jnp.exp(sink_ref[(2 * pair + par) * A_GROUP + g] - m))
            slabs.append(jnp.concatenate(parts, axis=1))
            sink_terms.append(jnp.where(low_out, terms[0], terms[1]))
        r = _dot(jnp.concatenate(slabs, axis=0), vcat)
        for g in range(A_GROUP):
            rows = slice(g * W, (g + 1) * W)
            out = r[rows, :LANES] * (1.0 / (r[rows, LANES:] + sink_terms[g]))
            o_ref[0, i * W:(i + 1) * W, base + g * LANES:base + (g + 1) * LANES] = out.astype(BF16)

    units = [(i, pair) for i in range(blocks) for pair in range(A_KV_HEADS // 2)]
    sc = scores(*units[0])
    for u, unit in enumerate(units):
        nxt = scores(*units[u + 1]) if u + 1 < len(units) else None
        finish(*unit, sc)
        sc = nxt


def _attn_call(q, k, v, sinks):
    b, s, d = q.shape
    W = WINDOW
    tq = ATTN_QUERIES
    kw = k.shape[2]
    per = tq // W
    cur = lambda w: pl.BlockSpec((1, tq, w), lambda i, j, *_: (i, j, 0))
    prev = lambda w: pl.BlockSpec((1, W, w), lambda i, j, *_: (i, jnp.maximum(j * per - 1, 0), 0))
    return pl.pallas_call(
        _attn_body,
        grid_spec=pltpu.PrefetchScalarGridSpec(
            num_scalar_prefetch=1,
            grid=(b, s // tq),
            in_specs=[cur(d), prev(kw), cur(kw), prev(kw), cur(kw)],
            out_specs=cur(d),
            scratch_shapes=[pltpu.VMEM((tq + W, kw), BF16), pltpu.VMEM((tq + W, kw), BF16)]),
        out_shape=jax.ShapeDtypeStruct((b, s, d), BF16),
        compiler_params=pltpu.CompilerParams(
            dimension_semantics=("parallel", "parallel"), vmem_limit_bytes=VMEM_LIMIT),
        name="swa_attn",
    )(sinks, q, k, k, v, v)


def kernel(x, p, positions, a_w_in, a_b_igate, a_b_fgate, a_head_norm_g, a_w_out, kv_w, kv_b, b_w_q, b_b_q, b_sinks, b_w_o, b_b_o, mix_ln_g, mix_ln_b, mlp_w_up, mlp_w_down, mlp_ln_g, mlp_ln_b, ple_w_gate, ple_b_gate, ple_w_proj):
    B, S, D = x.shape
    N = B * S
    row = lambda v: v.reshape(1, -1).astype(F32)
    rows = lambda v: v.reshape(DEPTH, 1, -1).astype(F32)

    def ffn(i, a, xs, wo, bo):
        return _ffn_call(i, a, xs, p.reshape(DEPTH, N, PLE_DIM), wo.astype(BF16), row(bo),
                         rows(mix_ln_g), rows(mix_ln_b), mlp_w_up.astype(BF16), mlp_w_down.astype(BF16),
                         rows(mlp_ln_g), rows(mlp_ln_b),
                         ple_w_gate.astype(BF16), rows(ple_b_gate), ple_w_proj.astype(BF16))

    w_t = jnp.pad(a_w_in[0].T, ((0, M_AUG - 2 * M_HEADS), (0, 0))).astype(BF16)
    bgr = jnp.concatenate([a_b_igate[0], a_b_fgate[0]]).reshape(2 * M_HEADS, 1).astype(F32)
    gain_b = jnp.broadcast_to(a_head_norm_g[0].astype(F32)[:, None], (D, LANES))
    qt, k, vt, ogt, b_rows, cm_rows, g_cols = _mlstm_proj_call(x, w_t, bgr, gain_b)
    hg = _mlstm_call(qt, k, vt, ogt, b_rows, cm_rows, g_cols)
    xs = ffn(0, hg.reshape(N, D), x.reshape(N, D), a_w_out[0], jnp.zeros((D,), F32))

    half = ROPE_DIM // 2
    inv_freq = jnp.power(ROPE_THETA, -jnp.arange(half, dtype=F32) * (2.0 / ROPE_DIM))
    q_scale = A_HEAD_DIM ** -0.5
    def by_head(w, axis):
        split = w.reshape(*w.shape[:axis], A_KV_HEADS // 2, 2, A_GROUP, A_HEAD_DIM, *w.shape[axis + 1:])
        return jnp.swapaxes(split, axis + 1, axis + 2).reshape(w.shape)
    wq = by_head(b_w_q[0] * q_scale, 1).astype(BF16)
    bq = row(by_head(b_b_q[0] * q_scale, 0))
    qr, kr, vr = _qkv_call(xs, positions.reshape(1, N), wq, bq, kv_w.astype(BF16), row(kv_b),
                           inv_freq.reshape(half, 1), _rope_selector())
    att = _attn_call(qr.reshape(B, S, D), kr.reshape(B, S, -1), vr.reshape(B, S, -1), b_sinks[0].astype(F32))
    xs = ffn(1, att.reshape(N, D), xs, by_head(b_w_o[0], 0), b_b_o[0])
    return xs.reshape(B, S, D)
```

```python
import functools

import jax
import jax.numpy as jnp
import numpy as np
from jax import lax
from jax.experimental import pallas as pl
from jax.experimental.pallas import tpu as pltpu

F32 = jnp.float32
BF16 = jnp.bfloat16

D_MODEL = 1024
DEPTH = 2
M_HEADS = 4
M_DV = D_MODEL // M_HEADS
M_DK = M_DV // 2
GATE_CAP = 15.0
A_HEAD_DIM = 64
A_Q_HEADS = D_MODEL // A_HEAD_DIM
A_KV_HEADS = 4
A_GROUP = A_Q_HEADS // A_KV_HEADS
WINDOW = 128
ROPE_DIM = A_HEAD_DIM // 4
ROPE_THETA = 500000.0
D_FF = 4 * D_MODEL
PLE_DIM = 256
LN_EPS = 1e-5
LOG2_E = 1.4426950408889634
DEEPNORM_ALPHA = (2 * DEPTH) ** 0.25

LANES = 128
VMEM_LIMIT = 56 * 1024 * 1024

FFN_TOKENS = 512
FFN_SUBTILES = 2
FF_CHUNK = 1024
PROJ_TOKENS = 512
M_BLOCK = 256
M_STEP_TOKENS = 2048
M_AUG = 16
QKV_Q_GROUP = 256
ATTN_QUERIES = 512

_NT = (((1,), (1,)), ((), ()))


def _dot(a, b):
    return jnp.dot(a, b, preferred_element_type=F32)


def _dot_nt(a, b):
    return lax.dot_general(a, b, _NT, preferred_element_type=F32)


def _resident(shape):
    zeros = (0,) * len(shape)
    return pl.BlockSpec(shape, lambda *_: zeros, pipeline_mode=pl.Buffered(1))


def _resident_part(block_shape, block_index):
    return pl.BlockSpec(block_shape, lambda *_: block_index, pipeline_mode=pl.Buffered(1))


def _layer_norm(y, g, b):
    mu = jnp.mean(y, axis=-1, keepdims=True)
    yc = y - mu
    var = jnp.mean(yc * yc, axis=-1, keepdims=True)
    return yc * lax.rsqrt(var + LN_EPS) * g + b


def _ffn_body(layer, a_ref, x_ref, p_ref, wo_ref, bo_ref, g1_ref, b1_ref, wup_ref, wdn_ref,
              g2_ref, b2_ref, wpg_ref, bpg_ref, wpp_ref, o_ref):
    sub = x_ref.shape[0] // FFN_SUBTILES
    rows = [slice(s * sub, (s + 1) * sub) for s in range(FFN_SUBTILES)]
    g1, b1, g2, b2, bpg = (r[layer:layer + 1, :] for r in (g1_ref, b1_ref, g2_ref, b2_ref, bpg_ref))

    def mlp(x1):
        x1b = x1.astype(BF16)
        acc = None
        for c in range(D_FF // FF_CHUNK):
            cols = slice(c * FF_CHUNK, (c + 1) * FF_CHUNK)
            h = jnp.maximum(_dot(x1b, wup_ref[:, cols]), 0.0)
            d = _dot((h * h).astype(BF16), wdn_ref[cols, :])
            acc = d if acc is None else acc + d
        return acc

    mix = [_dot(a_ref[r, :], wo_ref[...]) + bo_ref[...] for r in rows]
    x1 = [_layer_norm(DEEPNORM_ALPHA * x_ref[r, :] + m, g1, b1) for r, m in zip(rows, mix)]
    acc = [mlp(v) for v in x1]
    for r, v, a in zip(rows, x1, acc):
        x2 = _layer_norm(DEEPNORM_ALPHA * v + a, g2, b2)
        gate = jax.nn.sigmoid(_dot(x2.astype(BF16), wpg_ref[...]) + bpg)
        pe = _dot(p_ref[r, :].astype(BF16), wpp_ref[...])
        o_ref[r, :] = x2 + gate * pe


def _ffn_call(layer, a, x, p, wo, bo, g1, b1, wup, wdn, g2, b2, wpg, bpg, wpp):
    n, d = x.shape
    tm = FFN_TOKENS
    tok = lambda w: pl.BlockSpec((tm, w), lambda i: (i, 0))
    of_layer = lambda w: _resident_part((None,) + w.shape[1:], (layer,) + (0,) * (w.ndim - 1))
    return pl.pallas_call(
        functools.partial(_ffn_body, layer),
        grid=(n // tm,),
        in_specs=[tok(d), tok(d), pl.BlockSpec((None, tm, PLE_DIM), lambda i: (layer, i, 0)),
                  _resident(wo.shape), _resident(bo.shape), _resident(g1.shape), _resident(b1.shape),
                  of_layer(wup), of_layer(wdn), _resident(g2.shape), _resident(b2.shape),
                  of_layer(wpg), _resident(bpg.shape), of_layer(wpp)],
        out_specs=tok(d),
        out_shape=jax.ShapeDtypeStruct((n, d), F32),
        compiler_params=pltpu.CompilerParams(
            dimension_semantics=("parallel",), vmem_limit_bytes=VMEM_LIMIT),
        name="ffn",
    )(a, x, p, wo, bo, g1, b1, wup, wdn, g2, b2, wpg, bpg, wpp)


def _log_sigmoid(z):
    return jnp.minimum(z, 0.0) - jnp.log1p(jnp.exp(-jnp.abs(z)))


def _soft_cap(z):
    return GATE_CAP * jnp.tanh(z / GATE_CAP)


def _mlstm_layer_body(x_ref, wqt_ref, wkt_ref, wvt_ref, wot_ref, wgr_ref, bgr_ref, gain_ref,
                      o_ref, c_ref, m_ref):
    @pl.when(pl.program_id(1) == 0)
    def _():
        c_ref[...] = jnp.zeros_like(c_ref)
        m_ref[...] = jnp.zeros_like(m_ref)

    L = M_BLOCK
    reps = L // LANES
    blocks = x_ref.shape[1] // L
    keep_diag = (lax.broadcasted_iota(jnp.int32, (LANES, LANES), 0)
                 <= lax.broadcasted_iota(jnp.int32, (LANES, LANES), 1))
    ones_rows = jnp.ones((M_AUG, L), BF16)
    zeros_q = jnp.zeros((LANES, LANES), F32)
    gain = jnp.concatenate([gain_ref[...]] * reps, axis=1)
    lane = lax.broadcasted_iota(jnp.int32, (2 * M_HEADS, L), 1)

    def scan(v, combine, identity):
        shift = 1
        while shift < L:
            v = combine(v, jnp.where(lane >= shift, pltpu.roll(v, shift, 1), identity))
            shift *= 2
        return v

    def project(blk):
        tok = slice(blk * L, (blk + 1) * L)
        out = {}
        xb = x_ref[0, tok, :].astype(BF16)
        xtb = x_ref[0, tok, :].T.astype(BF16)
        out["k"] = _dot_nt(xb, wkt_ref[...]).astype(BF16)
        out["qt"] = (_dot(wqt_ref[...], xtb) * (M_DK ** -0.5)).astype(BF16)
        z = _soft_cap(_dot(wgr_ref[...], xtb)[0:2 * M_HEADS] + bgr_ref[...])
        yield out
        log_i = jnp.concatenate([z[:M_HEADS]] * 2, axis=0) * LOG2_E
        log_f = _log_sigmoid(jnp.concatenate([z[M_HEADS:]] * 2, axis=0)) * LOG2_E
        out["ogt"] = (jax.nn.sigmoid(_dot(wot_ref[...], xtb)) * gain).astype(BF16)
        out["vt"] = _dot(wvt_ref[...], xtb).astype(BF16)
        out["b"] = scan(log_f, jnp.add, 0.0)
        g = log_i - out["b"]
        out["cm"] = scan(g, jnp.maximum, -jnp.inf)
        out["gc"] = jnp.concatenate([g, jnp.zeros((LANES - 2 * M_HEADS, L), F32)], axis=0).T
        yield out

    def early(pr, h):
        qt = pr["qt"][h * M_DK:(h + 1) * M_DK, :]
        kh = pr["k"][:, h * M_DK:(h + 1) * M_DK]
        lhs = jnp.concatenate([pr["vt"][h * M_DV:(h + 1) * M_DV, :], ones_rows], axis=0)
        m_prev = m_ref[h]
        m_row = jnp.maximum(pr["cm"][h:h + 1], m_prev)
        m_last = m_row[:, L - 1:L]
        g_b = jnp.broadcast_to(pr["gc"][:, h:h + 1], (L, LANES))
        c_old = c_ref[h]
        st = _dot(kh, qt)
        cq = _dot(c_old.astype(BF16), qt)
        kw = (kh.astype(F32) * jnp.exp2(g_b - m_last)).astype(BF16)
        c_ref[h] = jnp.exp2(m_prev - m_last) * c_old + _dot(lhs, kw)
        m_ref[h] = pr["b"][h:h + 1, L - 1:L] + m_last
        return lhs, pr["b"][h:h + 1], m_prev, m_row, g_b, st, cq

    def late(blk, pr, h, lhs, b_row, m_prev, m_row, g_b, st, cq):
        sd_rows = []
        for i in range(reps):
            src_rows = slice(i * LANES, (i + 1) * LANES)
            quads = []
            for j in range(reps):
                tgt_lanes = slice(j * LANES, (j + 1) * LANES)
                if j < i:
                    quads.append(zeros_q)
                    continue
                dq = jnp.exp2(g_b[src_rows] - m_row[:, tgt_lanes])
                if j == i:
                    dq = jnp.where(keep_diag, dq, 0.0)
                quads.append(st[src_rows, tgt_lanes] * dq)
            sd_rows.append(jnp.concatenate(quads, axis=1))
        sd = jnp.concatenate(sd_rows, axis=0).astype(BF16)
        tot = _dot(lhs, sd) + cq * jnp.exp2(m_prev - m_row)
        num, den = tot[:M_DV], tot[M_DV:M_DV + 1]
        dmax = jnp.maximum(jnp.abs(den), jnp.exp2(-(b_row + m_row)))
        mu = jnp.mean(num, axis=0, keepdims=True)
        hc = num - mu
        var = jnp.mean(hc * hc, axis=0, keepdims=True)
        hn = hc * lax.rsqrt(var + LN_EPS * dmax * dmax)
        og = pr["ogt"][h * M_DV:(h + 1) * M_DV, :].astype(F32)
        o_ref[0, blk * L:(blk + 1) * L, h * M_DV:(h + 1) * M_DV] = (hn * og).T.astype(BF16)

    def recur(blk, pr):
        carried = [early(pr, h) for h in range(M_HEADS)]
        yield
        for h in range(M_HEADS // 2):
            late(blk, pr, h, *carried[h])
        yield
        for h in range(M_HEADS // 2, M_HEADS):
            late(blk, pr, h, *carried[h])
        yield

    pr = None
    for out in project(0):
        pr = out
    for blk in range(blocks):
        nxt_stages = project(blk + 1) if blk + 1 < blocks else iter(())
        nxt = None
        for _ in recur(blk, pr):
            nxt = next(nxt_stages, nxt)
        for out in nxt_stages:
            nxt = out
        pr = nxt


def _mlstm_layer_call(x, w_t, bgr, gain_b):
    b, s, d = x.shape
    t = M_STEP_TOKENS
    qk_w = M_HEADS * M_DK
    tok_major = pl.BlockSpec((1, t, d), lambda i, j: (i, j, 0))
    rows_of = lambda start, size: _resident_part((size, d), (start // size, 0))
    return pl.pallas_call(
        _mlstm_layer_body,
        grid=(b, s // t),
        in_specs=[tok_major, rows_of(0, qk_w), rows_of(qk_w, qk_w), rows_of(2 * qk_w, d),
                  rows_of(2 * qk_w + d, d), rows_of(2 * qk_w + 2 * d, M_AUG),
                  _resident(bgr.shape), _resident(gain_b.shape)],
        out_specs=tok_major,
        out_shape=jax.ShapeDtypeStruct((b, s, d), BF16),
        scratch_shapes=[pltpu.VMEM((M_HEADS, M_DV + M_AUG, M_DK), F32),
                        pltpu.VMEM((M_HEADS, 1, 1), F32)],
        compiler_params=pltpu.CompilerParams(
            dimension_semantics=("parallel", "arbitrary"), vmem_limit_bytes=VMEM_LIMIT),
        name="mlstm_layer",
    )(x, w_t, w_t, w_t, w_t, w_t, bgr, gain_b)


def _rope_selector():
    half = ROPE_DIM // 2
    sel = np.zeros((LANES, 3 * LANES), np.float32)
    for lane in range(LANES):
        dim, freq = lane % A_HEAD_DIM, lane % half
        if dim < ROPE_DIM:
            sel[freq, lane] = 1.0
        else:
            sel[2 * half, lane] = 1.0
        if dim < half:
            sel[half + freq, LANES + lane] = -1.0
        elif dim < ROPE_DIM:
            sel[half + freq, 2 * LANES + lane] = 1.0
    return jnp.asarray(np.concatenate([sel, sel], axis=0), BF16)


def _qkv_body(x_ref, pos_ref, wq_ref, bq_ref, wkv_ref, bkv_ref, invf_ref, sel_ref, q_ref, k_ref, v_ref):
    t = x_ref.shape[0]
    half = ROPE_DIM // 2
    xb = x_ref[...].astype(BF16)
    kv = _dot(xb, wkv_ref[...]) + bkv_ref[...]
    ang = invf_ref[...] * pos_ref[...].astype(F32)
    table = jnp.concatenate([jnp.cos(ang), jnp.sin(ang), jnp.ones_like(ang),
                             jnp.zeros((LANES - 3 * half, t), F32)], axis=0).T
    hi = table.astype(BF16)
    lo = (table - hi.astype(F32)).astype(BF16)
    coef = _dot(jnp.concatenate([hi, lo], axis=1), sel_ref[...])
    c_self, c_next, c_prev = coef[:, :LANES], coef[:, LANES:2 * LANES], coef[:, 2 * LANES:]

    def rope(z):
        return (z * c_self + pltpu.roll(z, LANES - half, 1) * c_next
                + pltpu.roll(z, half, 1) * c_prev)

    group = QKV_Q_GROUP
    q0 = _dot(xb, wq_ref[:, :group]) + bq_ref[:, :group]
    kv_half = kv.shape[1] // 2
    for c in range(kv_half // LANES):
        k_ref[:, c * LANES:(c + 1) * LANES] = rope(kv[:, c * LANES:(c + 1) * LANES]).astype(BF16)
    v_ref[...] = kv[:, kv_half:].astype(BF16)
    for g in range(wq_ref.shape[1] // group):
        nxt = None
        if (g + 1) * group < wq_ref.shape[1]:
            cols = slice((g + 1) * group, (g + 2) * group)
            nxt = _dot(xb, wq_ref[:, cols]) + bq_ref[:, cols]
        for c in range(group // LANES):
            q_ref[:, g * group + c * LANES:g * group + (c + 1) * LANES] = rope(
                q0[:, c * LANES:(c + 1) * LANES]).astype(BF16)
        q0 = nxt


def _qkv_call(x, pos, wq, bq, wkv, bkv, invf, sel):
    n, d = x.shape
    t = PROJ_TOKENS
    kw = wkv.shape[1] // 2
    tok = lambda w: pl.BlockSpec((t, w), lambda i: (i, 0))
    return pl.pallas_call(
        _qkv_body,
        grid=(n // t,),
        in_specs=[tok(d), pl.BlockSpec((1, t), lambda i: (0, i)), _resident(wq.shape), _resident(bq.shape),
                  _resident(wkv.shape), _resident(bkv.shape), _resident(invf.shape), _resident(sel.shape)],
        out_specs=[tok(d), tok(kw), tok(kw)],
        out_shape=[jax.ShapeDtypeStruct((n, d), BF16),
                   jax.ShapeDtypeStruct((n, kw), BF16),
                   jax.ShapeDtypeStruct((n, kw), BF16)],
        compiler_params=pltpu.CompilerParams(
            dimension_semantics=("parallel",), vmem_limit_bytes=VMEM_LIMIT),
        name="qkv_rope",
    )(x, pos, wq, bq, wkv, bkv, invf, sel)


def _attn_body(sink_ref, q_ref, kp_ref, kc_ref, vp_ref, vc_ref, o_ref, kf_ref, vf_ref):
    W = WINDOW
    tq = q_ref.shape[1]
    blocks = tq // W
    tile = pl.program_id(1)
    kf_ref[0:W] = kp_ref[0]
    kf_ref[W:] = kc_ref[0]
    vf_ref[0:W] = vp_ref[0]
    vf_ref[W:] = vc_ref[0]
    low = lax.broadcasted_iota(jnp.int32, (2 * W, LANES), 1) < A_HEAD_DIM
    keep_low = jnp.where(low, 1.0, 0.0).astype(BF16)
    keep_high = jnp.where(low, 0.0, 1.0).astype(BF16)
    from_prev = (lax.broadcasted_iota(jnp.int32, (W, W), 1)
                 > lax.broadcasted_iota(jnp.int32, (W, W), 0))
    keep_prev = jnp.where(from_prev, 1.0, 0.0).astype(BF16)
    keep_cur = jnp.where(from_prev, 0.0, 1.0).astype(BF16)
    ones_cols = jnp.concatenate([keep_low, keep_high], axis=0)
    low_out = lax.broadcasted_iota(jnp.int32, (W, LANES), 1) < A_HEAD_DIM
    no_prev = jnp.where(tile > 0, 0.0, -jnp.inf)

    def scores(i, pair):
        kband = kf_ref[i * W:(i + 2) * W, pair * LANES:(pair + 1) * LANES]
        kcat = jnp.concatenate([kband * keep_low, kband * keep_high], axis=0)
        base = pair * A_GROUP * LANES
        qs = jnp.concatenate([q_ref[0, i * W:(i + 1) * W, base + g * LANES:base + (g + 1) * LANES]
                              for g in range(A_GROUP)], axis=0)
        return _dot_nt(qs, kcat)

    def finish(i, pair, sc):
        vband = vf_ref[i * W:(i + 2) * W, pair * LANES:(pair + 1) * LANES]
        vcat = jnp.concatenate([jnp.concatenate([vband * keep_low, vband * keep_high], axis=0),
                                ones_cols], axis=1)
        base = pair * A_GROUP * LANES
        slabs, sink_terms = [], []
        for g in range(A_GROUP):
            parts, terms = [], []
            for par in range(2):
                s_h = sc[g * W:(g + 1) * W, par * 2 * W:(par + 1) * 2 * W]
                s_prev = s_h[:, :W] + no_prev if i == 0 else s_h[:, :W]
                c = jnp.where(from_prev, s_prev, s_h[:, W:])
                m = jnp.max(c, axis=1, keepdims=True)
                e = jnp.exp(c - m).astype(BF16)
                parts += [e * keep_prev, e * keep_cur]
                terms.append(jnp.exp(sink_ref[(2 * pair + par) * A_GROUP + g] - m))
            slabs.append(jnp.concatenate(parts, axis=1))
            sink_terms.append(jnp.where(low_out, terms[0], terms[1]))
        r = _dot(jnp.concatenate(slabs, axis=0), vcat)
        for g in range(A_GROUP):
            rows = slice(g * W, (g + 1) * W)
            out = r[rows, :LANES] * (1.0 / (r[rows, LANES:] + sink_terms[g]))
            o_ref[0, i * W:(i + 1) * W, base + g * LANES:base + (g + 1) * LANES] = out.astype(BF16)

    units = [(i, pair) for i in range(blocks) for pair in range(A_KV_HEADS // 2)]
    sc = scores(*units[0])
    for u, unit in enumerate(units):
        nxt = scores(*units[u + 1]) if u + 1 < len(units) else None
        finish(*unit, sc)
        sc = nxt


def _attn_call(q, k, v, sinks):
    b, s, d = q.shape
    W = WINDOW
    tq = ATTN_QUERIES
    kw = k.shape[2]
    per = tq // W
    cur = lambda w: pl.BlockSpec((1, tq, w), lambda i, j, *_: (i, j, 0))
    prev = lambda w: pl.BlockSpec((1, W, w), lambda i, j, *_: (i, jnp.maximum(j * per - 1, 0), 0))
    return pl.pallas_call(
        _attn_body,
        grid_spec=pltpu.PrefetchScalarGridSpec(
            num_scalar_prefetch=1,
            grid=(b, s // tq),
            in_specs=[cur(d), prev(kw), cur(kw), prev(kw), cur(kw)],
            out_specs=cur(d),
            scratch_shapes=[pltpu.VMEM((tq + W, kw), BF16), pltpu.VMEM((tq + W, kw), BF16)]),
        out_shape=jax.ShapeDtypeStruct((b, s, d), BF16),
        compiler_params=pltpu.CompilerParams(
            dimension_semantics=("parallel", "parallel"), vmem_limit_bytes=VMEM_LIMIT),
        name="swa_attn",
    )(sinks, q, k, k, v, v)


def kernel(x, p, positions, a_w_in, a_b_igate, a_b_fgate, a_head_norm_g, a_w_out, kv_w, kv_b, b_w_q, b_b_q, b_sinks, b_w_o, b_b_o, mix_ln_g, mix_ln_b, mlp_w_up, mlp_w_down, mlp_ln_g, mlp_ln_b, ple_w_gate, ple_b_gate, ple_w_proj):
    B, S, D = x.shape
    N = B * S
    row = lambda v: v.reshape(1, -1).astype(F32)

    def ffn(i, a, xs, wo, bo):
        return _ffn_call(i, a, xs, p.reshape(DEPTH, N, PLE_DIM), wo.astype(BF16), row(bo),
                         mix_ln_g, mix_ln_b, mlp_w_up.astype(BF16), mlp_w_down.astype(BF16),
                         mlp_ln_g, mlp_ln_b, ple_w_gate.astype(BF16), ple_b_gate, ple_w_proj.astype(BF16))

    w_t = jnp.pad(a_w_in[0].T, ((0, M_AUG - 2 * M_HEADS), (0, 0))).astype(BF16)
    bgr = jnp.concatenate([a_b_igate[0], a_b_fgate[0]]).reshape(2 * M_HEADS, 1).astype(F32)
    gain_b = jnp.broadcast_to(a_head_norm_g[0].astype(F32)[:, None], (D, LANES))
    hg = _mlstm_layer_call(x, w_t, bgr, gain_b)
    xs = ffn(0, hg.reshape(N, D), x.reshape(N, D), a_w_out[0], jnp.zeros((D,), F32))

    half = ROPE_DIM // 2
    inv_freq = jnp.power(ROPE_THETA, -jnp.arange(half, dtype=F32) * (2.0 / ROPE_DIM))
    q_scale = A_HEAD_DIM ** -0.5
    def by_head(w, axis):
        split = w.reshape(*w.shape[:axis], A_KV_HEADS // 2, 2, A_GROUP, A_HEAD_DIM, *w.shape[axis + 1:])
        return jnp.swapaxes(split, axis + 1, axis + 2).reshape(w.shape)
    wq = by_head(b_w_q[0] * q_scale, 1).astype(BF16)
    bq = row(by_head(b_b_q[0] * q_scale, 0))
    qr, kr, vr = _qkv_call(xs, positions.reshape(1, N), wq, bq, kv_w.astype(BF16), row(kv_b),
                           inv_freq.reshape(half, 1), _rope_selector())
    att = _attn_call(qr.reshape(B, S, D), kr.reshape(B, S, -1), vr.reshape(B, S, -1), b_sinks[0].astype(F32))
    xs = ffn(1, att.reshape(N, D), xs, by_head(b_w_o[0], 0), b_b_o[0])
    return xs.reshape(B, S, D)
```

```python
import functools

import jax
import jax.numpy as jnp
import numpy as np
from jax import lax
from jax.experimental import pallas as pl
from jax.experimental.pallas import tpu as pltpu

F32 = jnp.float32
BF16 = jnp.bfloat16

D_MODEL = 1024
DEPTH = 2
M_HEADS = 4
M_DV = D_MODEL // M_HEADS
M_DK = M_DV // 2
GATE_CAP = 15.0
A_HEAD_DIM = 64
A_Q_HEADS = D_MODEL // A_HEAD_DIM
A_KV_HEADS = 4
A_GROUP = A_Q_HEADS // A_KV_HEADS
WINDOW = 128
ROPE_DIM = A_HEAD_DIM // 4
ROPE_THETA = 500000.0
D_FF = 4 * D_MODEL
PLE_DIM = 256
LN_EPS = 1e-5
LOG2_E = 1.4426950408889634
DEEPNORM_ALPHA = (2 * DEPTH) ** 0.25

LANES = 128
VMEM_LIMIT = 56 * 1024 * 1024

FFN_TOKENS = 512
FFN_SUBTILES = 2
FF_CHUNK = 1024
PROJ_TOKENS = 1024
M_BLOCK = 256
M_STEP_TOKENS = 1024
M_AUG = 16
QKV_Q_GROUP = 256
ATTN_QUERIES = 1024

_NT = (((1,), (1,)), ((), ()))


def _dot(a, b):
    return jnp.dot(a, b, preferred_element_type=F32)


def _dot_nt(a, b):
    return lax.dot_general(a, b, _NT, preferred_element_type=F32)


def _resident(shape):
    zeros = (0,) * len(shape)
    return pl.BlockSpec(shape, lambda *_: zeros, pipeline_mode=pl.Buffered(1))


def _resident_part(block_shape, block_index):
    return pl.BlockSpec(block_shape, lambda *_: block_index, pipeline_mode=pl.Buffered(1))


def _layer_norm(y, g, b):
    mu = jnp.mean(y, axis=-1, keepdims=True)
    yc = y - mu
    var = jnp.mean(yc * yc, axis=-1, keepdims=True)
    return yc * lax.rsqrt(var + LN_EPS) * g + b


def _ffn_body(layer, a_ref, x_ref, p_ref, wo_ref, bo_ref, g1_ref, b1_ref, wup_ref, wdn_ref,
              g2_ref, b2_ref, wpg_ref, bpg_ref, wpp_ref, o_ref):
    sub = x_ref.shape[0] // FFN_SUBTILES
    rows = [slice(s * sub, (s + 1) * sub) for s in range(FFN_SUBTILES)]
    g1, b1, g2, b2, bpg = (r[layer:layer + 1, :] for r in (g1_ref, b1_ref, g2_ref, b2_ref, bpg_ref))

    def mlp(x1):
        x1b = x1.astype(BF16)
        acc = None
        for c in range(D_FF // FF_CHUNK):
            cols = slice(c * FF_CHUNK, (c + 1) * FF_CHUNK)
            h = jnp.maximum(_dot(x1b, wup_ref[:, cols]), 0.0)
            d = _dot((h * h).astype(BF16), wdn_ref[cols, :])
            acc = d if acc is None else acc + d
        return acc

    mix = [_dot(a_ref[r, :], wo_ref[...]) + bo_ref[...] for r in rows]
    x1 = [_layer_norm(DEEPNORM_ALPHA * x_ref[r, :] + m, g1, b1) for r, m in zip(rows, mix)]
    acc = [mlp(v) for v in x1]
    for r, v, a in zip(rows, x1, acc):
        x2 = _layer_norm(DEEPNORM_ALPHA * v + a, g2, b2)
        gate = jax.nn.sigmoid(_dot(x2.astype(BF16), wpg_ref[...]) + bpg)
        pe = _dot(p_ref[r, :].astype(BF16), wpp_ref[...])
        o_ref[r, :] = x2 + gate * pe


def _ffn_call(layer, a, x, p, wo, bo, g1, b1, wup, wdn, g2, b2, wpg, bpg, wpp):
    n, d = x.shape
    tm = FFN_TOKENS
    tok = lambda w: pl.BlockSpec((tm, w), lambda i: (i, 0))
    of_layer = lambda w: _resident_part((None,) + w.shape[1:], (layer,) + (0,) * (w.ndim - 1))
    return pl.pallas_call(
        functools.partial(_ffn_body, layer),
        grid=(n // tm,),
        in_specs=[tok(d), tok(d), pl.BlockSpec((None, tm, PLE_DIM), lambda i: (layer, i, 0)),
                  _resident(wo.shape), _resident(bo.shape), _resident(g1.shape), _resident(b1.shape),
                  of_layer(wup), of_layer(wdn), _resident(g2.shape), _resident(b2.shape),
                  of_layer(wpg), _resident(bpg.shape), of_layer(wpp)],
        out_specs=tok(d),
        out_shape=jax.ShapeDtypeStruct((n, d), F32),
        compiler_params=pltpu.CompilerParams(
            dimension_semantics=("parallel",), vmem_limit_bytes=VMEM_LIMIT),
        name="ffn",
    )(a, x, p, wo, bo, g1, b1, wup, wdn, g2, b2, wpg, bpg, wpp)


def _log_sigmoid(z):
    return jnp.minimum(z, 0.0) - jnp.log1p(jnp.exp(-jnp.abs(z)))


def _soft_cap(z):
    return GATE_CAP * jnp.tanh(z / GATE_CAP)


def _mlstm_layer_body(x_ref, wqt_ref, wkt_ref, wvt_ref, wot_ref, wgr_ref, bgr_ref, gain_ref,
                      o_ref, c_ref, m_ref):
    @pl.when(pl.program_id(1) == 0)
    def _():
        c_ref[...] = jnp.zeros_like(c_ref)
        m_ref[...] = jnp.zeros_like(m_ref)

    L = M_BLOCK
    reps = L // LANES
    blocks = x_ref.shape[1] // L
    keep_diag = (lax.broadcasted_iota(jnp.int32, (LANES, LANES), 0)
                 <= lax.broadcasted_iota(jnp.int32, (LANES, LANES), 1))
    ones_rows = jnp.ones((M_AUG, L), BF16)
    zeros_q = jnp.zeros((LANES, LANES), F32)
    gain = jnp.concatenate([gain_ref[...]] * reps, axis=1)
    lane = lax.broadcasted_iota(jnp.int32, (2 * M_HEADS, L), 1)

    def scan(v, combine, identity):
        shift = 1
        while shift < L:
            v = combine(v, jnp.where(lane >= shift, pltpu.roll(v, shift, 1), identity))
            shift *= 2
        return v

    def project(blk):
        tok = slice(blk * L, (blk + 1) * L)
        out = {}
        xb = x_ref[0, tok, :].astype(BF16)
        xtb = x_ref[0, tok, :].T.astype(BF16)
        out["k"] = _dot_nt(xb, wkt_ref[...]).astype(BF16)
        out["qt"] = (_dot(wqt_ref[...], xtb) * (M_DK ** -0.5)).astype(BF16)
        z = _soft_cap(_dot(wgr_ref[...], xtb)[0:2 * M_HEADS] + bgr_ref[...])
        yield out
        log_i = jnp.concatenate([z[:M_HEADS]] * 2, axis=0) * LOG2_E
        log_f = _log_sigmoid(jnp.concatenate([z[M_HEADS:]] * 2, axis=0)) * LOG2_E
        out["ogt"] = (jax.nn.sigmoid(_dot(wot_ref[...], xtb)) * gain).astype(BF16)
        out["vt"] = _dot(wvt_ref[...], xtb).astype(BF16)
        out["b"] = scan(log_f, jnp.add, 0.0)
        g = log_i - out["b"]
        out["cm"] = scan(g, jnp.maximum, -jnp.inf)
        out["gc"] = jnp.concatenate([g, jnp.zeros((LANES - 2 * M_HEADS, L), F32)], axis=0).T
        yield out

    def early(pr, h):
        qt = pr["qt"][h * M_DK:(h + 1) * M_DK, :]
        kh = pr["k"][:, h * M_DK:(h + 1) * M_DK]
        lhs = jnp.concatenate([pr["vt"][h * M_DV:(h + 1) * M_DV, :], ones_rows], axis=0)
        m_prev = m_ref[h]
        m_row = jnp.maximum(pr["cm"][h:h + 1], m_prev)
        m_last = m_row[:, L - 1:L]
        g_b = jnp.broadcast_to(pr["gc"][:, h:h + 1], (L, LANES))
        c_old = c_ref[h]
        st = _dot(kh, qt)
        cq = _dot(c_old.astype(BF16), qt)
        kw = (kh.astype(F32) * jnp.exp2(g_b - m_last)).astype(BF16)
        c_ref[h] = jnp.exp2(m_prev - m_last) * c_old + _dot(lhs, kw)
        m_ref[h] = pr["b"][h:h + 1, L - 1:L] + m_last
        return lhs, pr["b"][h:h + 1], m_prev, m_row, g_b, st, cq

    def late(blk, pr, h, lhs, b_row, m_prev, m_row, g_b, st, cq):
        sd_rows = []
        for i in range(reps):
            src_rows = slice(i * LANES, (i + 1) * LANES)
            quads = []
            for j in range(reps):
                tgt_lanes = slice(j * LANES, (j + 1) * LANES)
                if j < i:
                    quads.append(zeros_q)
                    continue
                dq = jnp.exp2(g_b[src_rows] - m_row[:, tgt_lanes])
                if j == i:
                    dq = jnp.where(keep_diag, dq, 0.0)
                quads.append(st[src_rows, tgt_lanes] * dq)
            sd_rows.append(jnp.concatenate(quads, axis=1))
        sd = jnp.concatenate(sd_rows, axis=0).astype(BF16)
        tot = _dot(lhs, sd) + cq * jnp.exp2(m_prev - m_row)
        num, den = tot[:M_DV], tot[M_DV:M_DV + 1]
        dmax = jnp.maximum(jnp.abs(den), jnp.exp2(-(b_row + m_row)))
        mu = jnp.mean(num, axis=0, keepdims=True)
        hc = num - mu
        var = jnp.mean(hc * hc, axis=0, keepdims=True)
        hn = hc * lax.rsqrt(var + LN_EPS * dmax * dmax)
        og = pr["ogt"][h * M_DV:(h + 1) * M_DV, :].astype(F32)
        o_ref[0, blk * L:(blk + 1) * L, h * M_DV:(h + 1) * M_DV] = (hn * og).T.astype(BF16)

    def recur(blk, pr):
        carried = [early(pr, h) for h in range(M_HEADS)]
        yield
        for h in range(M_HEADS // 2):
            late(blk, pr, h, *carried[h])
        yield
        for h in range(M_HEADS // 2, M_HEADS):
            late(blk, pr, h, *carried[h])
        yield

    pr = None
    for out in project(0):
        pr = out
    for blk in range(blocks):
        nxt_stages = project(blk + 1) if blk + 1 < blocks else iter(())
        nxt = None
        for _ in recur(blk, pr):
            nxt = next(nxt_stages, nxt)
        for out in nxt_stages:
            nxt = out
        pr = nxt


def _mlstm_layer_call(x, w_t, bgr, gain_b):
    b, s, d = x.shape
    t = M_STEP_TOKENS
    qk_w = M_HEADS * M_DK
    tok_major = pl.BlockSpec((1, t, d), lambda i, j: (i, j, 0))
    rows_of = lambda start, size: _resident_part((size, d), (start // size, 0))
    return pl.pallas_call(
        _mlstm_layer_body,
        grid=(b, s // t),
        in_specs=[tok_major, rows_of(0, qk_w), rows_of(qk_w, qk_w), rows_of(2 * qk_w, d),
                  rows_of(2 * qk_w + d, d), rows_of(2 * qk_w + 2 * d, M_AUG),
                  _resident(bgr.shape), _resident(gain_b.shape)],
        out_specs=tok_major,
        out_shape=jax.ShapeDtypeStruct((b, s, d), BF16),
        scratch_shapes=[pltpu.VMEM((M_HEADS, M_DV + M_AUG, M_DK), F32),
                        pltpu.VMEM((M_HEADS, 1, 1), F32)],
        compiler_params=pltpu.CompilerParams(
            dimension_semantics=("parallel", "arbitrary"), vmem_limit_bytes=VMEM_LIMIT),
        name="mlstm_layer",
    )(x, w_t, w_t, w_t, w_t, w_t, bgr, gain_b)


def _rope_selector():
    half = ROPE_DIM // 2
    sel = np.zeros((LANES, 3 * LANES), np.float32)
    for lane in range(LANES):
        dim, freq = lane % A_HEAD_DIM, lane % half
        if dim < ROPE_DIM:
            sel[freq, lane] = 1.0
        else:
            sel[2 * half, lane] = 1.0
        if dim < half:
            sel[half + freq, LANES + lane] = -1.0
        elif dim < ROPE_DIM:
            sel[half + freq, 2 * LANES + lane] = 1.0
    return jnp.asarray(np.concatenate([sel, sel], axis=0), BF16)


def _qkv_body(x_ref, pos_ref, wq_ref, bq_ref, wkv_ref, bkv_ref, invf_ref, sel_ref, q_ref, k_ref, v_ref):
    t = x_ref.shape[0]
    half = ROPE_DIM // 2
    xb = x_ref[...].astype(BF16)
    kv = _dot(xb, wkv_ref[...]) + bkv_ref[...]
    ang = invf_ref[...] * pos_ref[...].astype(F32)
    table = jnp.concatenate([jnp.cos(ang), jnp.sin(ang), jnp.ones_like(ang),
                             jnp.zeros((LANES - 3 * half, t), F32)], axis=0).T
    hi = table.astype(BF16)
    lo = (table - hi.astype(F32)).astype(BF16)
    coef = _dot(jnp.concatenate([hi, lo], axis=1), sel_ref[...])
    c_self, c_next, c_prev = coef[:, :LANES], coef[:, LANES:2 * LANES], coef[:, 2 * LANES:]

    def rope(z):
        return (z * c_self + pltpu.roll(z, LANES - half, 1) * c_next
                + pltpu.roll(z, half, 1) * c_prev)

    group = QKV_Q_GROUP
    q0 = _dot(xb, wq_ref[:, :group]) + bq_ref[:, :group]
    kv_half = kv.shape[1] // 2
    for c in range(kv_half // LANES):
        k_ref[:, c * LANES:(c + 1) * LANES] = rope(kv[:, c * LANES:(c + 1) * LANES]).astype(BF16)
    v_ref[...] = kv[:, kv_half:].astype(BF16)
    for g in range(wq_ref.shape[1] // group):
        nxt = None
        if (g + 1) * group < wq_ref.shape[1]:
            cols = slice((g + 1) * group, (g + 2) * group)
            nxt = _dot(xb, wq_ref[:, cols]) + bq_ref[:, cols]
        for c in range(group // LANES):
            q_ref[:, g * group + c * LANES:g * group + (c + 1) * LANES] = rope(
                q0[:, c * LANES:(c + 1) * LANES]).astype(BF16)
        q0 = nxt


def _qkv_call(x, pos, wq, bq, wkv, bkv, invf, sel):
    n, d = x.shape
    t = PROJ_TOKENS
    kw = wkv.shape[1] // 2
    tok = lambda w: pl.BlockSpec((t, w), lambda i: (i, 0))
    return pl.pallas_call(
        _qkv_body,
        grid=(n // t,),
        in_specs=[tok(d), pl.BlockSpec((1, t), lambda i: (0, i)), _resident(wq.shape), _resident(bq.shape),
                  _resident(wkv.shape), _resident(bkv.shape), _resident(invf.shape), _resident(sel.shape)],
        out_specs=[tok(d), tok(kw), tok(kw)],
        out_shape=[jax.ShapeDtypeStruct((n, d), BF16),
                   jax.ShapeDtypeStruct((n, kw), BF16),
                   jax.ShapeDtypeStruct((n, kw), BF16)],
        compiler_params=pltpu.CompilerParams(
            dimension_semantics=("parallel",), vmem_limit_bytes=VMEM_LIMIT),
        name="qkv_rope",
    )(x, pos, wq, bq, wkv, bkv, invf, sel)


def _attn_body(sink_ref, q_ref, kp_ref, kc_ref, vp_ref, vc_ref, o_ref, kf_ref, vf_ref):
    W = WINDOW
    tq = q_ref.shape[1]
    blocks = tq // W
    tile = pl.program_id(1)
    kf_ref[0:W] = kp_ref[0]
    kf_ref[W:] = kc_ref[0]
    vf_ref[0:W] = vp_ref[0]
    vf_ref[W:] = vc_ref[0]
    low = lax.broadcasted_iota(jnp.int32, (2 * W, LANES), 1) < A_HEAD_DIM
    keep_low = jnp.where(low, 1.0, 0.0).astype(BF16)
    keep_high = jnp.where(low, 0.0, 1.0).astype(BF16)
    from_prev = (lax.broadcasted_iota(jnp.int32, (W, W), 1)
                 > lax.broadcasted_iota(jnp.int32, (W, W), 0))
    keep_prev = jnp.where(from_prev, 1.0, 0.0).astype(BF16)
    keep_cur = jnp.where(from_prev, 0.0, 1.0).astype(BF16)
    ones_cols = jnp.concatenate([keep_low, keep_high], axis=0)
    low_out = lax.broadcasted_iota(jnp.int32, (W, LANES), 1) < A_HEAD_DIM
    no_prev = jnp.where(tile > 0, 0.0, -jnp.inf)

    def scores(i, pair):
        kband = kf_ref[i * W:(i + 2) * W, pair * LANES:(pair + 1) * LANES]
        kcat = jnp.concatenate([kband * keep_low, kband * keep_high], axis=0)
        base = pair * A_GROUP * LANES
        qs = jnp.concatenate([q_ref[0, i * W:(i + 1) * W, base + g * LANES:base + (g + 1) * LANES]
                              for g in range(A_GROUP)], axis=0)
        return _dot_nt(qs, kcat)

    def finish(i, pair, sc):
        vband = vf_ref[i * W:(i + 2) * W, pair * LANES:(pair + 1) * LANES]
        vcat = jnp.concatenate([jnp.concatenate([vband * keep_low, vband * keep_high], axis=0),
                                ones_cols], axis=1)
        base = pair * A_GROUP * LANES
        slabs, sink_terms = [], []
        for g in range(A_GROUP):
            parts, terms = [], []
            for par in range(2):
                s_h = sc[g * W:(g + 1) * W, par * 2 * W:(par + 1) * 2 * W]
                s_prev = s_h[:, :W] + no_prev if i == 0 else s_h[:, :W]
                c = jnp.where(from_prev, s_prev, s_h[:, W:])
                m = jnp.max(c, axis=1, keepdims=True)
                e = jnp.exp(c - m).astype(BF16)
                parts += [e * keep_prev, e * keep_cur]
                terms.append(jnp.exp(sink_ref[(2 * pair + par) * A_GROUP + g] - m))
            slabs.append(jnp.concatenate(parts, axis=1))
            sink_terms.append(jnp.where(low_out, terms[0], terms[1]))
        r = _dot(jnp.concatenate(slabs, axis=0), vcat)
        for g in range(A_GROUP):
            rows = slice(g * W, (g + 1) * W)
            out = r[rows, :LANES] * (1.0 / (r[rows, LANES:] + sink_terms[g]))
            o_ref[0, i * W:(i + 1) * W, base + g * LANES:base + (g + 1) * LANES] = out.astype(BF16)

    units = [(i, pair) for i in range(blocks) for pair in range(A_KV_HEADS // 2)]
    sc = scores(*units[0])
    for u, unit in enumerate(units):
        nxt = scores(*units[u + 1]) if u + 1 < len(units) else None
        finish(*unit, sc)
        sc = nxt


def _attn_call(q, k, v, sinks):
    b, s, d = q.shape
    W = WINDOW
    tq = ATTN_QUERIES
    kw = k.shape[2]
    per = tq // W
    cur = lambda w: pl.BlockSpec((1, tq, w), lambda i, j, *_: (i, j, 0))
    prev = lambda w: pl.BlockSpec((1, W, w), lambda i, j, *_: (i, jnp.maximum(j * per - 1, 0), 0))
    return pl.pallas_call(
        _attn_body,
        grid_spec=pltpu.PrefetchScalarGridSpec(
            num_scalar_prefetch=1,
            grid=(b, s // tq),
            in_specs=[cur(d), prev(kw), cur(kw), prev(kw), cur(kw)],
            out_specs=cur(d),
            scratch_shapes=[pltpu.VMEM((tq + W, kw), BF16), pltpu.VMEM((tq + W, kw), BF16)]),
        out_shape=jax.ShapeDtypeStruct((b, s, d), BF16),
        compiler_params=pltpu.CompilerParams(
            dimension_semantics=("parallel", "parallel"), vmem_limit_bytes=VMEM_LIMIT),
        name="swa_attn",
    )(sinks, q, k, k, v, v)


def kernel(x, p, positions, a_w_in, a_b_igate, a_b_fgate, a_head_norm_g, a_w_out, kv_w, kv_b, b_w_q, b_b_q, b_sinks, b_w_o, b_b_o, mix_ln_g, mix_ln_b, mlp_w_up, mlp_w_down, mlp_ln_g, mlp_ln_b, ple_w_gate, ple_b_gate, ple_w_proj):
    B, S, D = x.shape
    N = B * S
    row = lambda v: v.reshape(1, -1).astype(F32)

    def ffn(i, a, xs, wo, bo):
        return _ffn_call(i, a, xs, p.reshape(DEPTH, N, PLE_DIM), wo.astype(BF16), row(bo),
                         mix_ln_g, mix_ln_b, mlp_w_up.astype(BF16), mlp_w_down.astype(BF16),
                         mlp_ln_g, mlp_ln_b, ple_w_gate.astype(BF16), ple_b_gate, ple_w_proj.astype(BF16))

    w_t = jnp.pad(a_w_in[0].T, ((0, M_AUG - 2 * M_HEADS), (0, 0))).astype(BF16)
    bgr = jnp.concatenate([a_b_igate[0], a_b_fgate[0]]).reshape(2 * M_HEADS, 1).astype(F32)
    gain_b = jnp.broadcast_to(a_head_norm_g[0].astype(F32)[:, None], (D, LANES))
    hg = _mlstm_layer_call(x, w_t, bgr, gain_b)
    xs = ffn(0, hg.reshape(N, D), x.reshape(N, D), a_w_out[0], jnp.zeros((D,), F32))

    half = ROPE_DIM // 2
    inv_freq = jnp.power(ROPE_THETA, -jnp.arange(half, dtype=F32) * (2.0 / ROPE_DIM))
    q_scale = A_HEAD_DIM ** -0.5
    def by_head(w, axis):
        split = w.reshape(*w.shape[:axis], A_KV_HEADS // 2, 2, A_GROUP, A_HEAD_DIM, *w.shape[axis + 1:])
        return jnp.swapaxes(split, axis + 1, axis + 2).reshape(w.shape)
    wq = by_head(b_w_q[0] * q_scale, 1).astype(BF16)
    bq = row(by_head(b_b_q[0] * q_scale, 0))
    qr, kr, vr = _qkv_call(xs, positions.reshape(1, N), wq, bq, kv_w.astype(BF16), row(kv_b),
                           inv_freq.reshape(half, 1), _rope_selector())
    att = _attn_call(qr.reshape(B, S, D), kr.reshape(B, S, -1), vr.reshape(B, S, -1), b_sinks[0].astype(F32))
    xs = ffn(1, att.reshape(N, D), xs, by_head(b_w_o[0], 0), b_b_o[0])
    return xs.reshape(B, S, D)
```

```python
import functools

import jax
import jax.numpy as jnp
import numpy as np
from jax import lax
from jax.experimental import pallas as pl
from jax.experimental.pallas import tpu as pltpu

F32 = jnp.float32
BF16 = jnp.bfloat16

D_MODEL = 1024
DEPTH = 2
M_HEADS = 4
M_DV = D_MODEL // M_HEADS
M_DK = M_DV // 2
GATE_CAP = 15.0
A_HEAD_DIM = 64
A_Q_HEADS = D_MODEL // A_HEAD_DIM
A_KV_HEADS = 4
A_GROUP = A_Q_HEADS // A_KV_HEADS
WINDOW = 128
ROPE_DIM = A_HEAD_DIM // 4
ROPE_THETA = 500000.0
D_FF = 4 * D_MODEL
PLE_DIM = 256
LN_EPS = 1e-5
LOG2_E = 1.4426950408889634
DEEPNORM_ALPHA = (2 * DEPTH) ** 0.25

LANES = 128
VMEM_LIMIT = 56 * 1024 * 1024

FFN_TOKENS = 512
FFN_SUBTILES = 2
FF_CHUNK = 1024
PROJ_TOKENS = 1024
M_BLOCK = 256
M_STEP_TOKENS = 1024
M_AUG = 16
QKV_Q_GROUP = 256
ATTN_QUERIES = 2048

_NT = (((1,), (1,)), ((), ()))


def _dot(a, b):
    return jnp.dot(a, b, preferred_element_type=F32)


def _dot_nt(a, b):
    return lax.dot_general(a, b, _NT, preferred_element_type=F32)


def _resident(shape):
    zeros = (0,) * len(shape)
    return pl.BlockSpec(shape, lambda *_: zeros, pipeline_mode=pl.Buffered(1))


def _resident_part(block_shape, block_index):
    return pl.BlockSpec(block_shape, lambda *_: block_index, pipeline_mode=pl.Buffered(1))


def _layer_norm(y, g, b):
    mu = jnp.mean(y, axis=-1, keepdims=True)
    yc = y - mu
    var = jnp.mean(yc * yc, axis=-1, keepdims=True)
    return yc * lax.rsqrt(var + LN_EPS) * g + b


def _ffn_body(layer, a_ref, x_ref, p_ref, wo_ref, bo_ref, g1_ref, b1_ref, wup_ref, wdn_ref,
              g2_ref, b2_ref, wpg_ref, bpg_ref, wpp_ref, o_ref):
    sub = x_ref.shape[0] // FFN_SUBTILES
    rows = [slice(s * sub, (s + 1) * sub) for s in range(FFN_SUBTILES)]
    g1, b1, g2, b2, bpg = (r[layer:layer + 1, :] for r in (g1_ref, b1_ref, g2_ref, b2_ref, bpg_ref))

    def mlp(x1):
        x1b = x1.astype(BF16)
        acc = None
        for c in range(D_FF // FF_CHUNK):
            cols = slice(c * FF_CHUNK, (c + 1) * FF_CHUNK)
            h = jnp.maximum(_dot(x1b, wup_ref[:, cols]), 0.0)
            d = _dot((h * h).astype(BF16), wdn_ref[cols, :])
            acc = d if acc is None else acc + d
        return acc

    mix = [_dot(a_ref[r, :], wo_ref[...]) + bo_ref[...] for r in rows]
    x1 = [_layer_norm(DEEPNORM_ALPHA * x_ref[r, :] + m, g1, b1) for r, m in zip(rows, mix)]
    acc = [mlp(v) for v in x1]
    for r, v, a in zip(rows, x1, acc):
        x2 = _layer_norm(DEEPNORM_ALPHA * v + a, g2, b2)
        gate = jax.nn.sigmoid(_dot(x2.astype(BF16), wpg_ref[...]) + bpg)
        pe = _dot(p_ref[r, :].astype(BF16), wpp_ref[...])
        o_ref[r, :] = x2 + gate * pe


def _ffn_call(layer, a, x, p, wo, bo, g1, b1, wup, wdn, g2, b2, wpg, bpg, wpp):
    n, d = x.shape
    tm = FFN_TOKENS
    tok = lambda w: pl.BlockSpec((tm, w), lambda i: (i, 0))
    of_layer = lambda w: _resident_part((None,) + w.shape[1:], (layer,) + (0,) * (w.ndim - 1))
    return pl.pallas_call(
        functools.partial(_ffn_body, layer),
        grid=(n // tm,),
        in_specs=[tok(d), tok(d), pl.BlockSpec((None, tm, PLE_DIM), lambda i: (layer, i, 0)),
                  _resident(wo.shape), _resident(bo.shape), _resident(g1.shape), _resident(b1.shape),
                  of_layer(wup), of_layer(wdn), _resident(g2.shape), _resident(b2.shape),
                  of_layer(wpg), _resident(bpg.shape), of_layer(wpp)],
        out_specs=tok(d),
        out_shape=jax.ShapeDtypeStruct((n, d), F32),
        compiler_params=pltpu.CompilerParams(
            dimension_semantics=("parallel",), vmem_limit_bytes=VMEM_LIMIT),
        name="ffn",
    )(a, x, p, wo, bo, g1, b1, wup, wdn, g2, b2, wpg, bpg, wpp)


def _log_sigmoid(z):
    return jnp.minimum(z, 0.0) - jnp.log1p(jnp.exp(-jnp.abs(z)))


def _soft_cap(z):
    return GATE_CAP * jnp.tanh(z / GATE_CAP)


def _mlstm_layer_body(n_casts, x_ref, wqt_ref, wkt_ref, wvt_ref, wot_ref, wgr_ref, bgr_ref, gain_ref, *refs):
    cast_in, (o_ref, *cast_out), (c_ref, m_ref) = refs[:n_casts], refs[n_casts:2 * n_casts + 1], refs[2 * n_casts + 1:]

    @pl.when(pl.program_id(1) == 0)
    def _():
        c_ref[...] = jnp.zeros_like(c_ref)
        m_ref[...] = jnp.zeros_like(m_ref)

    for src, dst in zip(cast_in, cast_out):
        dst[...] = src[...].astype(BF16)

    L = M_BLOCK
    reps = L // LANES
    blocks = x_ref.shape[1] // L
    keep_diag = (lax.broadcasted_iota(jnp.int32, (LANES, LANES), 0)
                 <= lax.broadcasted_iota(jnp.int32, (LANES, LANES), 1))
    ones_rows = jnp.ones((M_AUG, L), BF16)
    zeros_q = jnp.zeros((LANES, LANES), F32)
    gain = jnp.concatenate([gain_ref[...]] * reps, axis=1)
    lane = lax.broadcasted_iota(jnp.int32, (2 * M_HEADS, L), 1)

    def scan(v, combine, identity):
        shift = 1
        while shift < L:
            v = combine(v, jnp.where(lane >= shift, pltpu.roll(v, shift, 1), identity))
            shift *= 2
        return v

    def project(blk):
        tok = slice(blk * L, (blk + 1) * L)
        out = {}
        xb = x_ref[0, tok, :].astype(BF16)
        xtb = x_ref[0, tok, :].T.astype(BF16)
        out["k"] = _dot_nt(xb, wkt_ref[...]).astype(BF16)
        out["qt"] = (_dot(wqt_ref[...], xtb) * (M_DK ** -0.5)).astype(BF16)
        z = _soft_cap(_dot(wgr_ref[...], xtb)[0:2 * M_HEADS] + bgr_ref[...])
        yield out
        log_i = jnp.concatenate([z[:M_HEADS]] * 2, axis=0) * LOG2_E
        log_f = _log_sigmoid(jnp.concatenate([z[M_HEADS:]] * 2, axis=0)) * LOG2_E
        out["ogt"] = (jax.nn.sigmoid(_dot(wot_ref[...], xtb)) * gain).astype(BF16)
        out["vt"] = _dot(wvt_ref[...], xtb).astype(BF16)
        out["b"] = scan(log_f, jnp.add, 0.0)
        g = log_i - out["b"]
        out["cm"] = scan(g, jnp.maximum, -jnp.inf)
        out["gc"] = jnp.concatenate([g, jnp.zeros((LANES - 2 * M_HEADS, L), F32)], axis=0).T
        yield out

    def early(pr, h):
        qt = pr["qt"][h * M_DK:(h + 1) * M_DK, :]
        kh = pr["k"][:, h * M_DK:(h + 1) * M_DK]
        lhs = jnp.concatenate([pr["vt"][h * M_DV:(h + 1) * M_DV, :], ones_rows], axis=0)
        m_prev = m_ref[h]
        m_row = jnp.maximum(pr["cm"][h:h + 1], m_prev)
        m_last = m_row[:, L - 1:L]
        g_b = jnp.broadcast_to(pr["gc"][:, h:h + 1], (L, LANES))
        c_old = c_ref[h]
        st = _dot(kh, qt)
        cq = _dot(c_old.astype(BF16), qt)
        kw = (kh.astype(F32) * jnp.exp2(g_b - m_last)).astype(BF16)
        c_ref[h] = jnp.exp2(m_prev - m_last) * c_old + _dot(lhs, kw)
        m_ref[h] = pr["b"][h:h + 1, L - 1:L] + m_last
        return lhs, pr["b"][h:h + 1], m_prev, m_row, g_b, st, cq

    def late(blk, pr, h, lhs, b_row, m_prev, m_row, g_b, st, cq):
        sd_rows = []
        for i in range(reps):
            src_rows = slice(i * LANES, (i + 1) * LANES)
            quads = []
            for j in range(reps):
                tgt_lanes = slice(j * LANES, (j + 1) * LANES)
                if j < i:
                    quads.append(zeros_q)
                    continue
                dq = jnp.exp2(g_b[src_rows] - m_row[:, tgt_lanes])
                if j == i:
                    dq = jnp.where(keep_diag, dq, 0.0)
                quads.append(st[src_rows, tgt_lanes] * dq)
            sd_rows.append(jnp.concatenate(quads, axis=1))
        sd = jnp.concatenate(sd_rows, axis=0).astype(BF16)
        tot = _dot(lhs, sd) + cq * jnp.exp2(m_prev - m_row)
        num, den = tot[:M_DV], tot[M_DV:M_DV + 1]
        dmax = jnp.maximum(jnp.abs(den), jnp.exp2(-(b_row + m_row)))
        mu = jnp.mean(num, axis=0, keepdims=True)
        hc = num - mu
        var = jnp.mean(hc * hc, axis=0, keepdims=True)
        hn = hc * lax.rsqrt(var + LN_EPS * dmax * dmax)
        og = pr["ogt"][h * M_DV:(h + 1) * M_DV, :].astype(F32)
        o_ref[0, blk * L:(blk + 1) * L, h * M_DV:(h + 1) * M_DV] = (hn * og).T.astype(BF16)

    def recur(blk, pr):
        carried = [early(pr, h) for h in range(M_HEADS)]
        yield
        for h in range(M_HEADS // 2):
            late(blk, pr, h, *carried[h])
        yield
        for h in range(M_HEADS // 2, M_HEADS):
            late(blk, pr, h, *carried[h])
        yield

    pr = None
    for out in project(0):
        pr = out
    for blk in range(blocks):
        nxt_stages = project(blk + 1) if blk + 1 < blocks else iter(())
        nxt = None
        for _ in recur(blk, pr):
            nxt = next(nxt_stages, nxt)
        for out in nxt_stages:
            nxt = out
        pr = nxt


def _mlstm_layer_call(x, w_t, bgr, gain_b, to_cast):
    b, s, d = x.shape
    t = M_STEP_TOKENS
    qk_w = M_HEADS * M_DK
    steps = s // t
    tok_major = pl.BlockSpec((1, t, d), lambda i, j: (i, j, 0))
    rows_of = lambda start, size: _resident_part((size, d), (start // size, 0))
    flat = [w.reshape(-1, w.shape[-1]) for w in to_cast]
    slices = [pl.BlockSpec((w.shape[0] // (b * steps), w.shape[1]), lambda i, j: (i * steps + j, 0)) for w in flat]
    outs = pl.pallas_call(
        functools.partial(_mlstm_layer_body, len(flat)),
        grid=(b, steps),
        in_specs=[tok_major, rows_of(0, qk_w), rows_of(qk_w, qk_w), rows_of(2 * qk_w, d),
                  rows_of(2 * qk_w + d, d), rows_of(2 * qk_w + 2 * d, M_AUG),
                  _resident(bgr.shape), _resident(gain_b.shape)] + slices,
        out_specs=[tok_major] + slices,
        out_shape=[jax.ShapeDtypeStruct((b, s, d), BF16)] + [jax.ShapeDtypeStruct(w.shape, BF16) for w in flat],
        scratch_shapes=[pltpu.VMEM((M_HEADS, M_DV + M_AUG, M_DK), F32),
                        pltpu.VMEM((M_HEADS, 1, 1), F32)],
        compiler_params=pltpu.CompilerParams(
            dimension_semantics=("parallel", "arbitrary"), vmem_limit_bytes=VMEM_LIMIT),
        name="mlstm_layer",
    )(x, w_t, w_t, w_t, w_t, w_t, bgr, gain_b, *flat)
    return outs[0], [o.reshape(w.shape) for o, w in zip(outs[1:], to_cast)]


_ROPE_HALF = ROPE_DIM // 2
_REST = A_HEAD_DIM - ROPE_DIM


def _slab_lanes(w):
    slabs = w.reshape(*w.shape[:-1], -1, 2, A_HEAD_DIM)
    a, b = slabs[..., 0, :], slabs[..., 1, :]
    t1 = lambda h: h[..., :_ROPE_HALF]
    t2 = lambda h: h[..., _ROPE_HALF:ROPE_DIM]
    rest = lambda h: h[..., ROPE_DIM:]
    return jnp.concatenate([t1(a), t1(b), rest(a), t2(a), t2(b), rest(b)], axis=-1).reshape(w.shape)


def _head_a_lanes(shape):
    lane = lax.broadcasted_iota(jnp.int32, shape, len(shape) - 1)
    in_t_group = jnp.where(lane % ROPE_DIM < _ROPE_HALF, 1.0, 0.0)
    in_rest = jnp.where(lane < A_HEAD_DIM + ROPE_DIM, 1.0, 0.0)
    return jnp.where(lane % A_HEAD_DIM < ROPE_DIM, in_t_group, in_rest)


def _rope_selector():
    sel = np.zeros((LANES, 2 * LANES), np.float32)
    for lane in range(LANES):
        freq, pos = lane % _ROPE_HALF, lane % A_HEAD_DIM
        if pos < ROPE_DIM:
            sel[freq, lane] = 1.0
            sel[_ROPE_HALF + freq, LANES + lane] = -1.0 if lane < A_HEAD_DIM else 1.0
        else:
            sel[2 * _ROPE_HALF, lane] = 1.0
    return jnp.asarray(np.concatenate([sel, sel], axis=0), BF16)


def _qkv_body(x_ref, pos_ref, wq_ref, bq_ref, wkv_ref, bkv_ref, invf_ref, sel_ref, q_ref, k_ref, v_ref):
    t = x_ref.shape[0]
    half = ROPE_DIM // 2
    xb = x_ref[...].astype(BF16)
    kv = _dot(xb, wkv_ref[...]) + bkv_ref[...]
    ang = invf_ref[...] * pos_ref[...].astype(F32)
    table = jnp.concatenate([jnp.cos(ang), jnp.sin(ang), jnp.ones_like(ang),
                             jnp.zeros((LANES - 3 * half, t), F32)], axis=0).T
    hi = table.astype(BF16)
    lo = (table - hi.astype(F32)).astype(BF16)
    coef = _dot(jnp.concatenate([hi, lo], axis=1), sel_ref[...])
    c_self, c_partner = coef[:, :LANES], coef[:, LANES:]

    def rope(z):
        return z * c_self + pltpu.roll(z, A_HEAD_DIM, 1) * c_partner

    group = QKV_Q_GROUP
    q0 = _dot(xb, wq_ref[:, :group]) + bq_ref[:, :group]
    kv_half = kv.shape[1] // 2
    for c in range(kv_half // LANES):
        k_ref[:, c * LANES:(c + 1) * LANES] = rope(kv[:, c * LANES:(c + 1) * LANES]).astype(BF16)
    v_ref[...] = kv[:, kv_half:].astype(BF16)
    for g in range(wq_ref.shape[1] // group):
        nxt = None
        if (g + 1) * group < wq_ref.shape[1]:
            cols = slice((g + 1) * group, (g + 2) * group)
            nxt = _dot(xb, wq_ref[:, cols]) + bq_ref[:, cols]
        for c in range(group // LANES):
            q_ref[:, g * group + c * LANES:g * group + (c + 1) * LANES] = rope(
                q0[:, c * LANES:(c + 1) * LANES]).astype(BF16)
        q0 = nxt


def _qkv_call(x, pos, wq, bq, wkv, bkv, invf, sel):
    n, d = x.shape
    t = PROJ_TOKENS
    kw = wkv.shape[1] // 2
    tok = lambda w: pl.BlockSpec((t, w), lambda i: (i, 0))
    return pl.pallas_call(
        _qkv_body,
        grid=(n // t,),
        in_specs=[tok(d), pl.BlockSpec((1, t), lambda i: (0, i)), _resident(wq.shape), _resident(bq.shape),
                  _resident(wkv.shape), _resident(bkv.shape), _resident(invf.shape), _resident(sel.shape)],
        out_specs=[tok(d), tok(kw), tok(kw)],
        out_shape=[jax.ShapeDtypeStruct((n, d), BF16),
                   jax.ShapeDtypeStruct((n, kw), BF16),
                   jax.ShapeDtypeStruct((n, kw), BF16)],
        compiler_params=pltpu.CompilerParams(
            dimension_semantics=("parallel",), vmem_limit_bytes=VMEM_LIMIT),
        name="qkv_rope",
    )(x, pos, wq, bq, wkv, bkv, invf, sel)


def _attn_body(sink_ref, q_ref, kp_ref, kc_ref, vp_ref, vc_ref, o_ref, kf_ref, vf_ref):
    W = WINDOW
    tq = q_ref.shape[1]
    blocks = tq // W
    tile = pl.program_id(1)
    kf_ref[0:W] = kp_ref[0]
    kf_ref[W:] = kc_ref[0]
    vf_ref[0:W] = vp_ref[0]
    vf_ref[W:] = vc_ref[0]
    low = lax.broadcasted_iota(jnp.int32, (2 * W, LANES), 1) < A_HEAD_DIM
    keep_low = jnp.where(low, 1.0, 0.0).astype(BF16)
    keep_high = jnp.where(low, 0.0, 1.0).astype(BF16)
    head_a = _head_a_lanes((2 * W, LANES))
    keep_a = head_a.astype(BF16)
    keep_b = (1.0 - head_a).astype(BF16)
    from_prev =(lax.broadcasted_iota(jnp.int32, (W, W), 1)
                 > lax.broadcasted_iota(jnp.int32, (W, W), 0))
    keep_prev = jnp.where(from_prev, 1.0, 0.0).astype(BF16)
    keep_cur = jnp.where(from_prev, 0.0, 1.0).astype(BF16)
    ones_cols = jnp.concatenate([keep_low, keep_high], axis=0)
    low_out = lax.broadcasted_iota(jnp.int32, (W, LANES), 1) < A_HEAD_DIM
    no_prev = jnp.where(tile > 0, 0.0, -jnp.inf)

    def scores(i, pair):
        kband = kf_ref[i * W:(i + 2) * W, pair * LANES:(pair + 1) * LANES]
        kcat = jnp.concatenate([kband * keep_a, kband * keep_b], axis=0)
        base = pair * A_GROUP * LANES
        qs = jnp.concatenate([q_ref[0, i * W:(i + 1) * W, base + g * LANES:base + (g + 1) * LANES]
                              for g in range(A_GROUP)], axis=0)
        return _dot_nt(qs, kcat)

    def finish(i, pair, sc):
        vband = vf_ref[i * W:(i + 2) * W, pair * LANES:(pair + 1) * LANES]
        vcat = jnp.concatenate([jnp.concatenate([vband * keep_low, vband * keep_high], axis=0),
                                ones_cols], axis=1)
        base = pair * A_GROUP * LANES
        slabs, sink_terms = [], []
        for g in range(A_GROUP):
            parts, terms = [], []
            for par in range(2):
                s_h = sc[g * W:(g + 1) * W, par * 2 * W:(par + 1) * 2 * W]
                s_prev = s_h[:, :W] + no_prev if i == 0 else s_h[:, :W]
                c = jnp.where(from_prev, s_prev, s_h[:, W:])
                m = jnp.max(c, axis=1, keepdims=True)
                e = jnp.exp(c - m).astype(BF16)
                parts += [e * keep_prev, e * keep_cur]
                terms.append(jnp.exp(sink_ref[(2 * pair + par) * A_GROUP + g] - m))
            slabs.append(jnp.concatenate(parts, axis=1))
            sink_terms.append(jnp.where(low_out, terms[0], terms[1]))
        r = _dot(jnp.concatenate(slabs, axis=0), vcat)
        for g in range(A_GROUP):
            rows = slice(g * W, (g + 1) * W)
            out = r[rows, :LANES] * (1.0 / (r[rows, LANES:] + sink_terms[g]))
            o_ref[0, i * W:(i + 1) * W, base + g * LANES:base + (g + 1) * LANES] = out.astype(BF16)

    units = [(i, pair) for i in range(blocks) for pair in range(A_KV_HEADS // 2)]
    sc = scores(*units[0])
    for u, unit in enumerate(units):
        nxt = scores(*units[u + 1]) if u + 1 < len(units) else None
        finish(*unit, sc)
        sc = nxt


def _attn_call(q, k, v, sinks):
    b, s, d = q.shape
    W = WINDOW
    tq = ATTN_QUERIES
    kw = k.shape[2]
    per = tq // W
    cur = lambda w: pl.BlockSpec((1, tq, w), lambda i, j, *_: (i, j, 0))
    prev = lambda w: pl.BlockSpec((1, W, w), lambda i, j, *_: (i, jnp.maximum(j * per - 1, 0), 0))
    return pl.pallas_call(
        _attn_body,
        grid_spec=pltpu.PrefetchScalarGridSpec(
            num_scalar_prefetch=1,
            grid=(b, s // tq),
            in_specs=[cur(d), prev(kw), cur(kw), prev(kw), cur(kw)],
            out_specs=cur(d),
            scratch_shapes=[pltpu.VMEM((tq + W, kw), BF16), pltpu.VMEM((tq + W, kw), BF16)]),
        out_shape=jax.ShapeDtypeStruct((b, s, d), BF16),
        compiler_params=pltpu.CompilerParams(
            dimension_semantics=("parallel", "parallel"), vmem_limit_bytes=VMEM_LIMIT),
        name="swa_attn",
    )(sinks, q, k, k, v, v)


def kernel(x, p, positions, a_w_in, a_b_igate, a_b_fgate, a_head_norm_g, a_w_out, kv_w, kv_b, b_w_q, b_b_q, b_sinks, b_w_o, b_b_o, mix_ln_g, mix_ln_b, mlp_w_up, mlp_w_down, mlp_ln_g, mlp_ln_b, ple_w_gate, ple_b_gate, ple_w_proj):
    B, S, D = x.shape
    N = B * S
    row = lambda v: v.reshape(1, -1).astype(F32)

    w_t = jnp.pad(a_w_in[0].T, ((0, M_AUG - 2 * M_HEADS), (0, 0))).astype(BF16)
    bgr = jnp.concatenate([a_b_igate[0], a_b_fgate[0]]).reshape(2 * M_HEADS, 1).astype(F32)
    gain_b = jnp.broadcast_to(a_head_norm_g[0].astype(F32)[:, None], (D, LANES))
    hg, (w_up, w_down, w_pgate, w_pproj, w_out0) = _mlstm_layer_call(
        x, w_t, bgr, gain_b, (mlp_w_up, mlp_w_down, ple_w_gate, ple_w_proj, a_w_out[0]))

    def ffn(i, a, xs, wo, bo):
        return _ffn_call(i, a, xs, p.reshape(DEPTH, N, PLE_DIM), wo, row(bo), mix_ln_g, mix_ln_b,
                         w_up, w_down, mlp_ln_g, mlp_ln_b, w_pgate, ple_b_gate, w_pproj)

    xs = ffn(0, hg.reshape(N, D), x.reshape(N, D), w_out0, jnp.zeros((D,), F32))

    half = ROPE_DIM // 2
    inv_freq = jnp.power(ROPE_THETA, -jnp.arange(half, dtype=F32) * (2.0 / ROPE_DIM))
    q_scale = A_HEAD_DIM ** -0.5
    def by_head(w, axis):
        split = w.reshape(*w.shape[:axis], A_KV_HEADS // 2, 2, A_GROUP, A_HEAD_DIM, *w.shape[axis + 1:])
        return jnp.swapaxes(split, axis + 1, axis + 2).reshape(w.shape)
    wq = _slab_lanes(by_head(b_w_q[0] * q_scale, 1)).astype(BF16)
    bq = row(_slab_lanes(by_head(b_b_q[0] * q_scale, 0)))
    kv_split = kv_w.shape[1] // 2
    wkv = jnp.concatenate([_slab_lanes(kv_w[:, :kv_split]), kv_w[:, kv_split:]], axis=1).astype(BF16)
    bkv = row(jnp.concatenate([_slab_lanes(kv_b[:kv_split]), kv_b[kv_split:]]))
    qr, kr, vr = _qkv_call(xs, positions.reshape(1, N), wq, bq, wkv, bkv,
                           inv_freq.reshape(half, 1), _rope_selector())
    att = _attn_call(qr.reshape(B, S, D), kr.reshape(B, S, -1), vr.reshape(B, S, -1), b_sinks[0].astype(F32))
    xs = ffn(1, att.reshape(N, D), xs, by_head(b_w_o[0], 0).astype(BF16), b_b_o[0])
    return xs.reshape(B, S, D)
```

```python
import functools

import jax
import jax.numpy as jnp
import numpy as np
from jax import lax
from jax.experimental import pallas as pl
from jax.experimental.pallas import tpu as pltpu

F32 = jnp.float32
BF16 = jnp.bfloat16

D_MODEL = 1024
DEPTH = 2
M_HEADS = 4
M_DV = D_MODEL // M_HEADS
M_DK = M_DV // 2
GATE_CAP = 15.0
A_HEAD_DIM = 64
A_Q_HEADS = D_MODEL // A_HEAD_DIM
A_KV_HEADS = 4
A_GROUP = A_Q_HEADS // A_KV_HEADS
WINDOW = 128
ROPE_DIM = A_HEAD_DIM // 4
ROPE_THETA = 500000.0
D_FF = 4 * D_MODEL
PLE_DIM = 256
LN_EPS = 1e-5
LOG2_E = 1.4426950408889634
DEEPNORM_ALPHA = (2 * DEPTH) ** 0.25

LANES = 128
VMEM_LIMIT = 56 * 1024 * 1024

FFN_TOKENS = 512
FFN_SUBTILES = 2
FF_CHUNK = 1024
PROJ_TOKENS = 1024
M_BLOCK = 256
M_STEP_TOKENS = 1024
M_AUG = 16
QKV_Q_GROUP = 256
QKV_TAIL_PARTS = 4
ATTN_QUERIES = 2048

_NT = (((1,), (1,)), ((), ()))


def _dot(a, b):
    return jnp.dot(a, b, preferred_element_type=F32)


def _dot_nt(a, b):
    return lax.dot_general(a, b, _NT, preferred_element_type=F32)


def _resident(shape):
    zeros = (0,) * len(shape)
    return pl.BlockSpec(shape, lambda *_: zeros, pipeline_mode=pl.Buffered(1))


def _resident_part(block_shape, block_index):
    return pl.BlockSpec(block_shape, lambda *_: block_index, pipeline_mode=pl.Buffered(1))


def _layer_norm(y, g, b):
    mu = jnp.mean(y, axis=-1, keepdims=True)
    yc = y - mu
    var = jnp.mean(yc * yc, axis=-1, keepdims=True)
    return yc * lax.rsqrt(var + LN_EPS) * g + b


def _ffn_body(layer, a_ref, x_ref, p_ref, wo_ref, bo_ref, g1_ref, b1_ref, wup_ref, wdn_ref,
              g2_ref, b2_ref, wpg_ref, bpg_ref, wpp_ref, o_ref):
    sub = x_ref.shape[0] // FFN_SUBTILES
    rows = [slice(s * sub, (s + 1) * sub) for s in range(FFN_SUBTILES)]
    g1, b1, g2, b2, bpg = (r[layer:layer + 1, :] for r in (g1_ref, b1_ref, g2_ref, b2_ref, bpg_ref))

    def mlp(x1):
        x1b = x1.astype(BF16)
        acc = None
        for c in range(D_FF // FF_CHUNK):
            cols = slice(c * FF_CHUNK, (c + 1) * FF_CHUNK)
            h = jnp.maximum(_dot(x1b, wup_ref[:, cols]), 0.0)
            d = _dot((h * h).astype(BF16), wdn_ref[cols, :])
            acc = d if acc is None else acc + d
        return acc

    mix = [_dot(a_ref[r, :], wo_ref[...]) + bo_ref[...] for r in rows]
    x1 = [_layer_norm(DEEPNORM_ALPHA * x_ref[r, :] + m, g1, b1) for r, m in zip(rows, mix)]
    acc = [mlp(v) for v in x1]
    for r, v, a in zip(rows, x1, acc):
        x2 = _layer_norm(DEEPNORM_ALPHA * v + a, g2, b2)
        gate = jax.nn.sigmoid(_dot(x2.astype(BF16), wpg_ref[...]) + bpg)
        pe = _dot(p_ref[r, :].astype(BF16), wpp_ref[...])
        o_ref[r, :] = x2 + gate * pe


def _ffn_call(layer, a, x, p, wo, bo, g1, b1, wup, wdn, g2, b2, wpg, bpg, wpp):
    n, d = x.shape
    tm = FFN_TOKENS
    tok = lambda w: pl.BlockSpec((tm, w), lambda i: (i, 0))
    of_layer = lambda w: _resident_part((None,) + w.shape[1:], (layer,) + (0,) * (w.ndim - 1))
    return pl.pallas_call(
        functools.partial(_ffn_body, layer),
        grid=(n // tm,),
        in_specs=[tok(d), tok(d), pl.BlockSpec((None, tm, PLE_DIM), lambda i: (layer, i, 0)),
                  _resident(wo.shape), _resident(bo.shape), _resident(g1.shape), _resident(b1.shape),
                  of_layer(wup), of_layer(wdn), _resident(g2.shape), _resident(b2.shape),
                  of_layer(wpg), _resident(bpg.shape), of_layer(wpp)],
        out_specs=tok(d),
        out_shape=jax.ShapeDtypeStruct((n, d), F32),
        compiler_params=pltpu.CompilerParams(
            dimension_semantics=("parallel",), vmem_limit_bytes=VMEM_LIMIT),
        name="ffn",
    )(a, x, p, wo, bo, g1, b1, wup, wdn, g2, b2, wpg, bpg, wpp)


def _log_sigmoid(z):
    return jnp.minimum(z, 0.0) - jnp.log1p(jnp.exp(-jnp.abs(z)))


def _soft_cap(z):
    return GATE_CAP * jnp.tanh(z / GATE_CAP)


def _mlstm_layer_body(n_casts, x_ref, wqt_ref, wkt_ref, wvt_ref, wot_ref, wgr_ref, bgr_ref, gain_ref, *refs):
    cast_in, (o_ref, *cast_out), (c_ref, m_ref) = refs[:n_casts], refs[n_casts:2 * n_casts + 1], refs[2 * n_casts + 1:]

    @pl.when(pl.program_id(1) == 0)
    def _():
        c_ref[...] = jnp.zeros_like(c_ref)
        m_ref[...] = jnp.zeros_like(m_ref)

    for src, dst in zip(cast_in, cast_out):
        dst[...] = src[...].astype(BF16)

    L = M_BLOCK
    reps = L // LANES
    blocks = x_ref.shape[1] // L
    keep_diag = (lax.broadcasted_iota(jnp.int32, (LANES, LANES), 0)
                 <= lax.broadcasted_iota(jnp.int32, (LANES, LANES), 1))
    ones_rows = jnp.ones((M_AUG, L), BF16)
    zeros_q = jnp.zeros((LANES, LANES), F32)
    gain = jnp.concatenate([gain_ref[...]] * reps, axis=1)
    lane = lax.broadcasted_iota(jnp.int32, (2 * M_HEADS, L), 1)

    def scan(v, combine, identity):
        shift = 1
        while shift < L:
            v = combine(v, jnp.where(lane >= shift, pltpu.roll(v, shift, 1), identity))
            shift *= 2
        return v

    def project(blk):
        tok = slice(blk * L, (blk + 1) * L)
        out = {}
        xb = x_ref[0, tok, :].astype(BF16)
        xtb = x_ref[0, tok, :].T.astype(BF16)
        out["k"] = _dot_nt(xb, wkt_ref[...]).astype(BF16)
        out["qt"] = (_dot(wqt_ref[...], xtb) * (M_DK ** -0.5)).astype(BF16)
        z = _soft_cap(_dot(wgr_ref[...], xtb)[0:2 * M_HEADS] + bgr_ref[...])
        yield out
        log_i = jnp.concatenate([z[:M_HEADS]] * 2, axis=0) * LOG2_E
        log_f = _log_sigmoid(jnp.concatenate([z[M_HEADS:]] * 2, axis=0)) * LOG2_E
        out["ogt"] = (jax.nn.sigmoid(_dot(wot_ref[...], xtb)) * gain).astype(BF16)
        out["vt"] = _dot(wvt_ref[...], xtb).astype(BF16)
        out["b"] = scan(log_f, jnp.add, 0.0)
        g = log_i - out["b"]
        out["cm"] = scan(g, jnp.maximum, -jnp.inf)
        out["gc"] = jnp.concatenate([g, jnp.zeros((LANES - 2 * M_HEADS, L), F32)], axis=0).T
        yield out

    def early(pr, h):
        qt = pr["qt"][h * M_DK:(h + 1) * M_DK, :]
        kh = pr["k"][:, h * M_DK:(h + 1) * M_DK]
        lhs = jnp.concatenate([pr["vt"][h * M_DV:(h + 1) * M_DV, :], ones_rows], axis=0)
        m_prev = m_ref[h]
        m_row = jnp.maximum(pr["cm"][h:h + 1], m_prev)
        m_last = m_row[:, L - 1:L]
        g_b = jnp.broadcast_to(pr["gc"][:, h:h + 1], (L, LANES))
        c_old = c_ref[h]
        st = _dot(kh, qt)
        cq = _dot(c_old.astype(BF16), qt)
        kw = (kh.astype(F32) * jnp.exp2(g_b - m_last)).astype(BF16)
        c_ref[h] = jnp.exp2(m_prev - m_last) * c_old + _dot(lhs, kw)
        m_ref[h] = pr["b"][h:h + 1, L - 1:L] + m_last
        return lhs, pr["b"][h:h + 1], m_prev, m_row, g_b, st, cq

    def late(blk, pr, h, lhs, b_row, m_prev, m_row, g_b, st, cq):
        sd_rows = []
        for i in range(reps):
            src_rows = slice(i * LANES, (i + 1) * LANES)
            quads = []
            for j in range(reps):
                tgt_lanes = slice(j * LANES, (j + 1) * LANES)
                if j < i:
                    quads.append(zeros_q)
                    continue
                dq = jnp.exp2(g_b[src_rows] - m_row[:, tgt_lanes])
                if j == i:
                    dq = jnp.where(keep_diag, dq, 0.0)
                quads.append(st[src_rows, tgt_lanes] * dq)
            sd_rows.append(jnp.concatenate(quads, axis=1))
        sd = jnp.concatenate(sd_rows, axis=0).astype(BF16)
        tot = _dot(lhs, sd) + cq * jnp.exp2(m_prev - m_row)
        num, den = tot[:M_DV], tot[M_DV:M_DV + 1]
        dmax = jnp.maximum(jnp.abs(den), jnp.exp2(-(b_row + m_row)))
        mu = jnp.mean(num, axis=0, keepdims=True)
        hc = num - mu
        var = jnp.mean(hc * hc, axis=0, keepdims=True)
        hn = hc * lax.rsqrt(var + LN_EPS * dmax * dmax)
        og = pr["ogt"][h * M_DV:(h + 1) * M_DV, :].astype(F32)
        o_ref[0, blk * L:(blk + 1) * L, h * M_DV:(h + 1) * M_DV] = (hn * og).T.astype(BF16)

    def recur(blk, pr):
        carried = [early(pr, h) for h in range(M_HEADS)]
        yield
        for h in range(M_HEADS // 2):
            late(blk, pr, h, *carried[h])
        yield
        for h in range(M_HEADS // 2, M_HEADS):
            late(blk, pr, h, *carried[h])
        yield

    pr = None
    for out in project(0):
        pr = out
    for blk in range(blocks):
        nxt_stages = project(blk + 1) if blk + 1 < blocks else iter(())
        nxt = None
        for _ in recur(blk, pr):
            nxt = next(nxt_stages, nxt)
        for out in nxt_stages:
            nxt = out
        pr = nxt


def _mlstm_layer_call(x, w_t, bgr, gain_b, to_cast):
    b, s, d = x.shape
    t = M_STEP_TOKENS
    qk_w = M_HEADS * M_DK
    steps = s // t
    tok_major = pl.BlockSpec((1, t, d), lambda i, j: (i, j, 0))
    rows_of = lambda start, size: _resident_part((size, d), (start // size, 0))
    flat = [w.reshape(-1, w.shape[-1]) for w in to_cast]
    slices = [pl.BlockSpec((w.shape[0] // (b * steps), w.shape[1]), lambda i, j: (i * steps + j, 0)) for w in flat]
    outs = pl.pallas_call(
        functools.partial(_mlstm_layer_body, len(flat)),
        grid=(b, steps),
        in_specs=[tok_major, rows_of(0, qk_w), rows_of(qk_w, qk_w), rows_of(2 * qk_w, d),
                  rows_of(2 * qk_w + d, d), rows_of(2 * qk_w + 2 * d, M_AUG),
                  _resident(bgr.shape), _resident(gain_b.shape)] + slices,
        out_specs=[tok_major] + slices,
        out_shape=[jax.ShapeDtypeStruct((b, s, d), BF16)] + [jax.ShapeDtypeStruct(w.shape, BF16) for w in flat],
        scratch_shapes=[pltpu.VMEM((M_HEADS, M_DV + M_AUG, M_DK), F32),
                        pltpu.VMEM((M_HEADS, 1, 1), F32)],
        compiler_params=pltpu.CompilerParams(
            dimension_semantics=("parallel", "arbitrary"), vmem_limit_bytes=VMEM_LIMIT),
        name="mlstm_layer",
    )(x, w_t, w_t, w_t, w_t, w_t, bgr, gain_b, *flat)
    return outs[0], [o.reshape(w.shape) for o, w in zip(outs[1:], to_cast)]


_ROPE_HALF = ROPE_DIM // 2
_REST = A_HEAD_DIM - ROPE_DIM


def _slab_lanes(w):
    slabs = w.reshape(*w.shape[:-1], -1, 2, A_HEAD_DIM)
    a, b = slabs[..., 0, :], slabs[..., 1, :]
    t1 = lambda h: h[..., :_ROPE_HALF]
    t2 = lambda h: h[..., _ROPE_HALF:ROPE_DIM]
    rest = lambda h: h[..., ROPE_DIM:]
    return jnp.concatenate([t1(a), t1(b), rest(a), t2(a), t2(b), rest(b)], axis=-1).reshape(w.shape)


def _head_a_lanes(shape):
    lane = lax.broadcasted_iota(jnp.int32, shape, len(shape) - 1)
    in_t_group = jnp.where(lane % ROPE_DIM < _ROPE_HALF, 1.0, 0.0)
    in_rest = jnp.where(lane < A_HEAD_DIM + ROPE_DIM, 1.0, 0.0)
    return jnp.where(lane % A_HEAD_DIM < ROPE_DIM, in_t_group, in_rest)


def _rope_selector():
    sel = np.zeros((LANES, 2 * LANES), np.float32)
    for lane in range(LANES):
        freq, pos = lane % _ROPE_HALF, lane % A_HEAD_DIM
        if pos < ROPE_DIM:
            sel[freq, lane] = 1.0
            sel[_ROPE_HALF + freq, LANES + lane] = -1.0 if lane < A_HEAD_DIM else 1.0
        else:
            sel[2 * _ROPE_HALF, lane] = 1.0
    return jnp.asarray(np.concatenate([sel, sel], axis=0), BF16)


def _qkv_body(x_ref, pos_ref, wq_ref, bq_ref, wkv_ref, bkv_ref, invf_ref, sel_ref, q_ref, k_ref, v_ref):
    t = x_ref.shape[0]
    half = ROPE_DIM // 2
    xb = x_ref[...].astype(BF16)
    kv = _dot(xb, wkv_ref[...]) + bkv_ref[...]
    ang = invf_ref[...] * pos_ref[...].astype(F32)
    table = jnp.concatenate([jnp.cos(ang), jnp.sin(ang), jnp.ones_like(ang),
                             jnp.zeros((LANES - 3 * half, t), F32)], axis=0).T
    hi = table.astype(BF16)
    lo = (table - hi.astype(F32)).astype(BF16)
    coef = _dot(jnp.concatenate([hi, lo], axis=1), sel_ref[...])
    c_self, c_partner = coef[:, :LANES], coef[:, LANES:]

    def rope(z, rows=slice(None)):
        return z * c_self[rows] + pltpu.roll(z, A_HEAD_DIM, 1) * c_partner[rows]

    group = QKV_Q_GROUP
    groups = wq_ref.shape[1] // group
    units = [(g, slice(r * t // parts, (r + 1) * t // parts))
             for g in range(groups) for parts in [QKV_TAIL_PARTS if g == groups - 1 else 1] for r in range(parts)]

    def project(g, rows):
        return _dot(xb[rows], wq_ref[:, g * group:(g + 1) * group]) + bq_ref[:, g * group:(g + 1) * group]

    q0 = project(*units[0])
    kv_half = kv.shape[1] // 2
    for c in range(kv_half // LANES):
        k_ref[:, c * LANES:(c + 1) * LANES] = rope(kv[:, c * LANES:(c + 1) * LANES]).astype(BF16)
    v_ref[...] = kv[:, kv_half:].astype(BF16)
    for u, (g, rows) in enumerate(units):
        nxt = project(*units[u + 1]) if u + 1 < len(units) else None
        for c in range(group // LANES):
            q_ref[rows, g * group + c * LANES:g * group + (c + 1) * LANES] = rope(
                q0[:, c * LANES:(c + 1) * LANES], rows).astype(BF16)
        q0 = nxt


def _qkv_call(x, pos, wq, bq, wkv, bkv, invf, sel):
    n, d = x.shape
    t = PROJ_TOKENS
    kw = wkv.shape[1] // 2
    tok = lambda w: pl.BlockSpec((t, w), lambda i: (i, 0))
    return pl.pallas_call(
        _qkv_body,
        grid=(n // t,),
        in_specs=[tok(d), pl.BlockSpec((1, t), lambda i: (0, i)), _resident(wq.shape), _resident(bq.shape),
                  _resident(wkv.shape), _resident(bkv.shape), _resident(invf.shape), _resident(sel.shape)],
        out_specs=[tok(d), tok(kw), tok(kw)],
        out_shape=[jax.ShapeDtypeStruct((n, d), BF16),
                   jax.ShapeDtypeStruct((n, kw), BF16),
                   jax.ShapeDtypeStruct((n, kw), BF16)],
        compiler_params=pltpu.CompilerParams(
            dimension_semantics=("parallel",), vmem_limit_bytes=VMEM_LIMIT),
        name="qkv_rope",
    )(x, pos, wq, bq, wkv, bkv, invf, sel)


def _attn_body(sink_ref, q_ref, kp_ref, kc_ref, vp_ref, vc_ref, o_ref, kf_ref, vf_ref):
    W = WINDOW
    tq = q_ref.shape[1]
    blocks = tq // W
    tile = pl.program_id(1)
    kf_ref[0:W] = kp_ref[0]
    kf_ref[W:] = kc_ref[0]
    vf_ref[0:W] = vp_ref[0]
    vf_ref[W:] = vc_ref[0]
    low = lax.broadcasted_iota(jnp.int32, (2 * W, LANES), 1) < A_HEAD_DIM
    keep_low = jnp.where(low, 1.0, 0.0).astype(BF16)
    keep_high = jnp.where(low, 0.0, 1.0).astype(BF16)
    head_a = _head_a_lanes((2 * W, LANES))
    keep_a = head_a.astype(BF16)
    keep_b = (1.0 - head_a).astype(BF16)
    from_prev =(lax.broadcasted_iota(jnp.int32, (W, W), 1)
                 > lax.broadcasted_iota(jnp.int32, (W, W), 0))
    keep_prev = jnp.where(from_prev, 1.0, 0.0).astype(BF16)
    keep_cur = jnp.where(from_prev, 0.0, 1.0).astype(BF16)
    ones_cols = jnp.concatenate([keep_low, keep_high], axis=0)
    low_out = lax.broadcasted_iota(jnp.int32, (W, LANES), 1) < A_HEAD_DIM
    no_prev = jnp.where(tile > 0, 0.0, -jnp.inf)

    def scores(i, pair):
        kband = kf_ref[i * W:(i + 2) * W, pair * LANES:(pair + 1) * LANES]
        kcat = jnp.concatenate([kband * keep_a, kband * keep_b], axis=0)
        base = pair * A_GROUP * LANES
        qs = jnp.concatenate([q_ref[0, i * W:(i + 1) * W, base + g * LANES:base + (g + 1) * LANES]
                              for g in range(A_GROUP)], axis=0)
        return _dot_nt(qs, kcat)

    def finish(i, pair, sc):
        vband = vf_ref[i * W:(i + 2) * W, pair * LANES:(pair + 1) * LANES]
        vcat = jnp.concatenate([jnp.concatenate([vband * keep_low, vband * keep_high], axis=0),
                                ones_cols], axis=1)
        base = pair * A_GROUP * LANES
        slabs, sink_terms = [], []
        for g in range(A_GROUP):
            parts, terms = [], []
            for par in range(2):
                s_h = sc[g * W:(g + 1) * W, par * 2 * W:(par + 1) * 2 * W]
                s_prev = s_h[:, :W] + no_prev if i == 0 else s_h[:, :W]
                c = jnp.where(from_prev, s_prev, s_h[:, W:])
                m = jnp.max(c, axis=1, keepdims=True)
                e = jnp.exp(c - m).astype(BF16)
                parts += [e * keep_prev, e * keep_cur]
                terms.append(jnp.exp(sink_ref[(2 * pair + par) * A_GROUP + g] - m))
            slabs.append(jnp.concatenate(parts, axis=1))
            sink_terms.append(jnp.where(low_out, terms[0], terms[1]))
        r = _dot(jnp.concatenate(slabs, axis=0), vcat)
        for g in range(A_GROUP):
            rows = slice(g * W, (g + 1) * W)
            out = r[rows, :LANES] * (1.0 / (r[rows, LANES:] + sink_terms[g]))
            o_ref[0, i * W:(i + 1) * W, base + g * LANES:base + (g + 1) * LANES] = out.astype(BF16)

    units = [(i, pair) for i in range(blocks) for pair in range(A_KV_HEADS // 2)]
    sc = scores(*units[0])
    for u, unit in enumerate(units):
        nxt = scores(*units[u + 1]) if u + 1 < len(units) else None
        finish(*unit, sc)
        sc = nxt


def _attn_call(q, k, v, sinks):
    b, s, d = q.shape
    W = WINDOW
    tq = ATTN_QUERIES
    kw = k.shape[2]
    per = tq // W
    cur = lambda w: pl.BlockSpec((1, tq, w), lambda i, j, *_: (i, j, 0))
    prev = lambda w: pl.BlockSpec((1, W, w), lambda i, j, *_: (i, jnp.maximum(j * per - 1, 0), 0))
    return pl.pallas_call(
        _attn_body,
        grid_spec=pltpu.PrefetchScalarGridSpec(
            num_scalar_prefetch=1,
            grid=(b, s // tq),
            in_specs=[cur(d), prev(kw), cur(kw), prev(kw), cur(kw)],
            out_specs=cur(d),
            scratch_shapes=[pltpu.VMEM((tq + W, kw), BF16), pltpu.VMEM((tq + W, kw), BF16)]),
        out_shape=jax.ShapeDtypeStruct((b, s, d), BF16),
        compiler_params=pltpu.CompilerParams(
            dimension_semantics=("parallel", "parallel"), vmem_limit_bytes=VMEM_LIMIT),
        name="swa_attn",
    )(sinks, q, k, k, v, v)


def kernel(x, p, positions, a_w_in, a_b_igate, a_b_fgate, a_head_norm_g, a_w_out, kv_w, kv_b, b_w_q, b_b_q, b_sinks, b_w_o, b_b_o, mix_ln_g, mix_ln_b, mlp_w_up, mlp_w_down, mlp_ln_g, mlp_ln_b, ple_w_gate, ple_b_gate, ple_w_proj):
    B, S, D = x.shape
    N = B * S
    row = lambda v: v.reshape(1, -1).astype(F32)

    w_t = jnp.pad(a_w_in[0].T, ((0, M_AUG - 2 * M_HEADS), (0, 0))).astype(BF16)
    bgr = jnp.concatenate([a_b_igate[0], a_b_fgate[0]]).reshape(2 * M_HEADS, 1).astype(F32)
    gain_b = jnp.broadcast_to(a_head_norm_g[0].astype(F32)[:, None], (D, LANES))
    hg, (w_up, w_down, w_pgate, w_pproj, w_out0) = _mlstm_layer_call(
        x, w_t, bgr, gain_b, (mlp_w_up, mlp_w_down, ple_w_gate, ple_w_proj, a_w_out[0]))

    def ffn(i, a, xs, wo, bo):
        return _ffn_call(i, a, xs, p.reshape(DEPTH, N, PLE_DIM), wo, row(bo), mix_ln_g, mix_ln_b,
                         w_up, w_down, mlp_ln_g, mlp_ln_b, w_pgate, ple_b_gate, w_pproj)

    xs = ffn(0, hg.reshape(N, D), x.reshape(N, D), w_out0, jnp.zeros((D,), F32))

    half = ROPE_DIM // 2
    inv_freq = jnp.power(ROPE_THETA, -jnp.arange(half, dtype=F32) * (2.0 / ROPE_DIM))
    q_scale = A_HEAD_DIM ** -0.5
    def by_head(w, axis):
        split = w.reshape(*w.shape[:axis], A_KV_HEADS // 2, 2, A_GROUP, A_HEAD_DIM, *w.shape[axis + 1:])
        return jnp.swapaxes(split, axis + 1, axis + 2).reshape(w.shape)
    wq = _slab_lanes(by_head(b_w_q[0] * q_scale, 1)).astype(BF16)
    bq = row(_slab_lanes(by_head(b_b_q[0] * q_scale, 0)))
    kv_split = kv_w.shape[1] // 2
    wkv = jnp.concatenate([_slab_lanes(kv_w[:, :kv_split]), kv_w[:, kv_split:]], axis=1).astype(BF16)
    bkv = row(jnp.concatenate([_slab_lanes(kv_b[:kv_split]), kv_b[kv_split:]]))
    qr, kr, vr = _qkv_call(xs, positions.reshape(1, N), wq, bq, wkv, bkv,
                           inv_freq.reshape(half, 1), _rope_selector())
    att = _attn_call(qr.reshape(B, S, D), kr.reshape(B, S, -1), vr.reshape(B, S, -1), b_sinks[0].astype(F32))
    xs = ffn(1, att.reshape(N, D), xs, by_head(b_w_o[0], 0).astype(BF16), b_b_o[0])
    return xs.reshape(B, S, D)
```

```python
import functools

import jax
import jax.numpy as jnp
import numpy as np
from jax import lax
from jax.experimental import pallas as pl
from jax.experimental.pallas import tpu as pltpu

F32 = jnp.float32
BF16 = jnp.bfloat16

D_MODEL = 1024
DEPTH = 2
M_HEADS = 4
M_DV = D_MODEL // M_HEADS
M_DK = M_DV // 2
GATE_CAP = 15.0
A_HEAD_DIM = 64
A_Q_HEADS = D_MODEL // A_HEAD_DIM
A_KV_HEADS = 4
A_GROUP = A_Q_HEADS // A_KV_HEADS
WINDOW = 128
ROPE_DIM = A_HEAD_DIM // 4
ROPE_THETA = 500000.0
D_FF = 4 * D_MODEL
PLE_DIM = 256
LN_EPS = 1e-5
LOG2_E = 1.4426950408889634
DEEPNORM_ALPHA = (2 * DEPTH) ** 0.25

LANES = 128
VMEM_LIMIT = 56 * 1024 * 1024

FFN_TOKENS = 512
FFN_SUBTILES = 2
PROJ_TOKENS = 1024
M_BLOCK = 256
M_STEP_TOKENS = 1024
M_AUG = 16
QKV_Q_GROUP = 256
ATTN_QUERIES = 2048

_NT = (((1,), (1,)), ((), ()))


def _dot(a, b):
    return jnp.dot(a, b, preferred_element_type=F32)


def _dot_nt(a, b):
    return lax.dot_general(a, b, _NT, preferred_element_type=F32)


def _resident(shape):
    zeros = (0,) * len(shape)
    return pl.BlockSpec(shape, lambda *_: zeros, pipeline_mode=pl.Buffered(1))


def _resident_part(block_shape, block_index):
    return pl.BlockSpec(block_shape, lambda *_: block_index, pipeline_mode=pl.Buffered(1))


def _layer_norm(y, g, b):
    mu = jnp.mean(y, axis=-1, keepdims=True)
    yc = y - mu
    var = jnp.mean(yc * yc, axis=-1, keepdims=True)
    return yc * lax.rsqrt(var + LN_EPS) * g + b


def _ffn_body(layer, a_ref, x_ref, p_ref, wo_ref, bo_ref, g1_ref, b1_ref, wup_ref, wdn_ref,
              g2_ref, b2_ref, wpg_ref, bpg_ref, wpp_ref, o_ref):
    sub = x_ref.shape[0] // FFN_SUBTILES
    rows = [slice(s * sub, (s + 1) * sub) for s in range(FFN_SUBTILES)]
    g1, b1, g2, b2, bpg = (r[layer:layer + 1, :] for r in (g1_ref, b1_ref, g2_ref, b2_ref, bpg_ref))

    def mlp(x1):
        h = jnp.maximum(_dot(x1.astype(BF16), wup_ref[...]), 0.0)
        return _dot((h * h).astype(BF16), wdn_ref[...])

    mix = [_dot(a_ref[r, :], wo_ref[...]) + bo_ref[...] for r in rows]
    x1 = [_layer_norm(DEEPNORM_ALPHA * x_ref[r, :] + m, g1, b1) for r, m in zip(rows, mix)]
    acc = [mlp(v) for v in x1]
    for r, v, a in zip(rows, x1, acc):
        x2 = _layer_norm(DEEPNORM_ALPHA * v + a, g2, b2)
        gate = jax.nn.sigmoid(_dot(x2.astype(BF16), wpg_ref[...]) + bpg)
        pe = _dot(p_ref[r, :].astype(BF16), wpp_ref[...])
        o_ref[r, :] = x2 + gate * pe


def _ffn_call(layer, a, x, p, wo, bo, g1, b1, wup, wdn, g2, b2, wpg, bpg, wpp):
    n, d = x.shape
    tm = FFN_TOKENS
    tok = lambda w: pl.BlockSpec((tm, w), lambda i: (i, 0))
    of_layer = lambda w: _resident_part((None,) + w.shape[1:], (layer,) + (0,) * (w.ndim - 1))
    return pl.pallas_call(
        functools.partial(_ffn_body, layer),
        grid=(n // tm,),
        in_specs=[tok(d), tok(d), pl.BlockSpec((None, tm, PLE_DIM), lambda i: (layer, i, 0)),
                  _resident(wo.shape), _resident(bo.shape), _resident(g1.shape), _resident(b1.shape),
                  of_layer(wup), of_layer(wdn), _resident(g2.shape), _resident(b2.shape),
                  of_layer(wpg), _resident(bpg.shape), of_layer(wpp)],
        out_specs=tok(d),
        out_shape=jax.ShapeDtypeStruct((n, d), F32),
        compiler_params=pltpu.CompilerParams(
            dimension_semantics=("arbitrary",), vmem_limit_bytes=VMEM_LIMIT),
        name="ffn",
    )(a, x, p, wo, bo, g1, b1, wup, wdn, g2, b2, wpg, bpg, wpp)


def _log_sigmoid(z):
    return jnp.minimum(z, 0.0) - jnp.log1p(jnp.exp(-jnp.abs(z)))


def _soft_cap(z):
    return GATE_CAP * jnp.tanh(z / GATE_CAP)


def _mlstm_layer_body(n_casts, x_ref, wqt_ref, wkt_ref, wvt_ref, wot_ref, wgr_ref, bgr_ref, gain_ref, *refs):
    cast_in, (o_ref, *cast_out), (c_ref, m_ref) = refs[:n_casts], refs[n_casts:2 * n_casts + 1], refs[2 * n_casts + 1:]

    @pl.when(pl.program_id(1) == 0)
    def _():
        c_ref[...] = jnp.zeros_like(c_ref)
        m_ref[...] = jnp.zeros_like(m_ref)

    for src, dst in zip(cast_in, cast_out):
        dst[...] = src[...].astype(BF16)

    L = M_BLOCK
    reps = L // LANES
    blocks = x_ref.shape[1] // L
    keep_diag = (lax.broadcasted_iota(jnp.int32, (LANES, LANES), 0)
                 <= lax.broadcasted_iota(jnp.int32, (LANES, LANES), 1))
    ones_rows = jnp.ones((M_AUG, L), BF16)
    zeros_q = jnp.zeros((LANES, LANES), F32)
    gain = jnp.concatenate([gain_ref[...]] * reps, axis=1)
    lane = lax.broadcasted_iota(jnp.int32, (2 * M_HEADS, L), 1)

    def scan(v, combine, identity):
        shift = 1
        while shift < L:
            v = combine(v, jnp.where(lane >= shift, pltpu.roll(v, shift, 1), identity))
            shift *= 2
        return v

    def project(blk):
        tok = slice(blk * L, (blk + 1) * L)
        out = {}
        xb = x_ref[0, tok, :].astype(BF16)
        xtb = x_ref[0, tok, :].T.astype(BF16)
        out["k"] = _dot_nt(xb, wkt_ref[...]).astype(BF16)
        out["qt"] = (_dot(wqt_ref[...], xtb) * (M_DK ** -0.5)).astype(BF16)
        z = _soft_cap(_dot(wgr_ref[...], xtb)[0:2 * M_HEADS] + bgr_ref[...])
        yield out
        log_i = jnp.concatenate([z[:M_HEADS]] * 2, axis=0) * LOG2_E
        log_f = _log_sigmoid(jnp.concatenate([z[M_HEADS:]] * 2, axis=0)) * LOG2_E
        out["ogt"] = (jax.nn.sigmoid(_dot(wot_ref[...], xtb)) * gain).astype(BF16)
        out["vt"] = _dot(wvt_ref[...], xtb).astype(BF16)
        out["b"] = scan(log_f, jnp.add, 0.0)
        g = log_i - out["b"]
        out["cm"] = scan(g, jnp.maximum, -jnp.inf)
        out["gc"] = jnp.concatenate([g, jnp.zeros((LANES - 2 * M_HEADS, L), F32)], axis=0).T
        yield out

    def early(pr, h):
        qt = pr["qt"][h * M_DK:(h + 1) * M_DK, :]
        kh = pr["k"][:, h * M_DK:(h + 1) * M_DK]
        lhs = jnp.concatenate([pr["vt"][h * M_DV:(h + 1) * M_DV, :], ones_rows], axis=0)
        m_prev = m_ref[h]
        m_row = jnp.maximum(pr["cm"][h:h + 1], m_prev)
        m_last = m_row[:, L - 1:L]
        g_b = jnp.broadcast_to(pr["gc"][:, h:h + 1], (L, LANES))
        c_old = c_ref[h]
        st = _dot(kh, qt)
        cq = _dot(c_old.astype(BF16), qt)
        kw = (kh.astype(F32) * jnp.exp2(g_b - m_last)).astype(BF16)
        c_ref[h] = jnp.exp2(m_prev - m_last) * c_old + _dot(lhs, kw)
        m_ref[h] = pr["b"][h:h + 1, L - 1:L] + m_last
        return lhs, pr["b"][h:h + 1], m_prev, m_row, g_b, st, cq

    def late(blk, pr, h, lhs, b_row, m_prev, m_row, g_b, st, cq):
        sd_rows = []
        for i in range(reps):
            src_rows = slice(i * LANES, (i + 1) * LANES)
            quads = []
            for j in range(reps):
                tgt_lanes = slice(j * LANES, (j + 1) * LANES)
                if j < i:
                    quads.append(zeros_q)
                    continue
                dq = jnp.exp2(g_b[src_rows] - m_row[:, tgt_lanes])
                if j == i:
                    dq = jnp.where(keep_diag, dq, 0.0)
                quads.append(st[src_rows, tgt_lanes] * dq)
            sd_rows.append(jnp.concatenate(quads, axis=1))
        sd = jnp.concatenate(sd_rows, axis=0).astype(BF16)
        tot = _dot(lhs, sd) + cq * jnp.exp2(m_prev - m_row)
        num, den = tot[:M_DV], tot[M_DV:M_DV + 1]
        dmax = jnp.maximum(jnp.abs(den), jnp.exp2(-(b_row + m_row)))
        mu = jnp.mean(num, axis=0, keepdims=True)
        hc = num - mu
        var = jnp.mean(hc * hc, axis=0, keepdims=True)
        hn = hc * lax.rsqrt(var + LN_EPS * dmax * dmax)
        og = pr["ogt"][h * M_DV:(h + 1) * M_DV, :].astype(F32)
        o_ref[0, blk * L:(blk + 1) * L, h * M_DV:(h + 1) * M_DV] = (hn * og).T.astype(BF16)

    def recur(blk, pr):
        carried = [early(pr, h) for h in range(M_HEADS)]
        yield
        for h in range(M_HEADS // 2):
            late(blk, pr, h, *carried[h])
        yield
        for h in range(M_HEADS // 2, M_HEADS):
            late(blk, pr, h, *carried[h])
        yield

    pr = None
    for out in project(0):
        pr = out
    for blk in range(blocks):
        nxt_stages = project(blk + 1) if blk + 1 < blocks else iter(())
        nxt = None
        for _ in recur(blk, pr):
            nxt = next(nxt_stages, nxt)
        for out in nxt_stages:
            nxt = out
        pr = nxt


def _mlstm_layer_call(x, w_t, bgr, gain_b, to_cast):
    b, s, d = x.shape
    t = M_STEP_TOKENS
    qk_w = M_HEADS * M_DK
    steps = s // t
    tok_major = pl.BlockSpec((1, t, d), lambda i, j: (i, j, 0))
    rows_of = lambda start, size: _resident_part((size, d), (start // size, 0))
    flat = [w.reshape(-1, w.shape[-1]) for w in to_cast]
    slices = [pl.BlockSpec((w.shape[0] // (b * steps), w.shape[1]), lambda i, j: (i * steps + j, 0)) for w in flat]
    outs = pl.pallas_call(
        functools.partial(_mlstm_layer_body, len(flat)),
        grid=(b, steps),
        in_specs=[tok_major, rows_of(0, qk_w), rows_of(qk_w, qk_w), rows_of(2 * qk_w, d),
                  rows_of(2 * qk_w + d, d), rows_of(2 * qk_w + 2 * d, M_AUG),
                  _resident(bgr.shape), _resident(gain_b.shape)] + slices,
        out_specs=[tok_major] + slices,
        out_shape=[jax.ShapeDtypeStruct((b, s, d), BF16)] + [jax.ShapeDtypeStruct(w.shape, BF16) for w in flat],
        scratch_shapes=[pltpu.VMEM((M_HEADS, M_DV + M_AUG, M_DK), F32),
                        pltpu.VMEM((M_HEADS, 1, 1), F32)],
        compiler_params=pltpu.CompilerParams(
            dimension_semantics=("parallel", "arbitrary"), vmem_limit_bytes=VMEM_LIMIT),
        name="mlstm_layer",
    )(x, w_t, w_t, w_t, w_t, w_t, bgr, gain_b, *flat)
    return outs[0], [o.reshape(w.shape) for o, w in zip(outs[1:], to_cast)]


_ROPE_HALF = ROPE_DIM // 2
_REST = A_HEAD_DIM - ROPE_DIM


def _slab_lanes(w):
    slabs = w.reshape(*w.shape[:-1], -1, 2, A_HEAD_DIM)
    a, b = slabs[..., 0, :], slabs[..., 1, :]
    t1 = lambda h: h[..., :_ROPE_HALF]
    t2 = lambda h: h[..., _ROPE_HALF:ROPE_DIM]
    rest = lambda h: h[..., ROPE_DIM:]
    return jnp.concatenate([t1(a), t1(b), rest(a), t2(a), t2(b), rest(b)], axis=-1).reshape(w.shape)


def _head_a_lanes(shape):
    lane = lax.broadcasted_iota(jnp.int32, shape, len(shape) - 1)
    in_t_group = jnp.where(lane % ROPE_DIM < _ROPE_HALF, 1.0, 0.0)
    in_rest = jnp.where(lane < A_HEAD_DIM + ROPE_DIM, 1.0, 0.0)
    return jnp.where(lane % A_HEAD_DIM < ROPE_DIM, in_t_group, in_rest)


def _rope_selector():
    sel = np.zeros((LANES, 2 * LANES), np.float32)
    for lane in range(LANES):
        freq, pos = lane % _ROPE_HALF, lane % A_HEAD_DIM
        if pos < ROPE_DIM:
            sel[freq, lane] = 1.0
            sel[_ROPE_HALF + freq, LANES + lane] = -1.0 if lane < A_HEAD_DIM else 1.0
        else:
            sel[2 * _ROPE_HALF, lane] = 1.0
    return jnp.asarray(np.concatenate([sel, sel], axis=0), BF16)


def _qkv_body(x_ref, pos_ref, wq_ref, bq_ref, wkv_ref, bkv_ref, invf_ref, sel_ref, q_ref, k_ref, v_ref):
    t = x_ref.shape[0]
    half = ROPE_DIM // 2
    xb = x_ref[...].astype(BF16)
    kv = _dot(xb, wkv_ref[...]) + bkv_ref[...]
    ang = invf_ref[...] * pos_ref[...].astype(F32)
    table = jnp.concatenate([jnp.cos(ang), jnp.sin(ang), jnp.ones_like(ang),
                             jnp.zeros((LANES - 3 * half, t), F32)], axis=0).T
    hi = table.astype(BF16)
    lo = (table - hi.astype(F32)).astype(BF16)
    coef = _dot(jnp.concatenate([hi, lo], axis=1), sel_ref[...])
    c_self, c_partner = coef[:, :LANES], coef[:, LANES:]

    def rope(z):
        return z * c_self + pltpu.roll(z, A_HEAD_DIM, 1) * c_partner

    group = QKV_Q_GROUP
    q0 = _dot(xb, wq_ref[:, :group]) + bq_ref[:, :group]
    kv_half = kv.shape[1] // 2
    for c in range(kv_half // LANES):
        k_ref[:, c * LANES:(c + 1) * LANES] = rope(kv[:, c * LANES:(c + 1) * LANES]).astype(BF16)
    v_ref[...] = kv[:, kv_half:].astype(BF16)
    for g in range(wq_ref.shape[1] // group):
        nxt = None
        if (g + 1) * group < wq_ref.shape[1]:
            cols = slice((g + 1) * group, (g + 2) * group)
            nxt = _dot(xb, wq_ref[:, cols]) + bq_ref[:, cols]
        for c in range(group // LANES):
            q_ref[:, g * group + c * LANES:g * group + (c + 1) * LANES] = rope(
                q0[:, c * LANES:(c + 1) * LANES]).astype(BF16)
        q0 = nxt


def _qkv_call(x, pos, wq, bq, wkv, bkv, invf, sel):
    n, d = x.shape
    t = PROJ_TOKENS
    kw = wkv.shape[1] // 2
    tok = lambda w: pl.BlockSpec((t, w), lambda i: (i, 0))
    return pl.pallas_call(
        _qkv_body,
        grid=(n // t,),
        in_specs=[tok(d), pl.BlockSpec((1, t), lambda i: (0, i)), _resident(wq.shape), _resident(bq.shape),
                  _resident(wkv.shape), _resident(bkv.shape), _resident(invf.shape), _resident(sel.shape)],
        out_specs=[tok(d), tok(kw), tok(kw)],
        out_shape=[jax.ShapeDtypeStruct((n, d), BF16),
                   jax.ShapeDtypeStruct((n, kw), BF16),
                   jax.ShapeDtypeStruct((n, kw), BF16)],
        compiler_params=pltpu.CompilerParams(
            dimension_semantics=("parallel",), vmem_limit_bytes=VMEM_LIMIT),
        name="qkv_rope",
    )(x, pos, wq, bq, wkv, bkv, invf, sel)


def _attn_body(sink_ref, q_ref, kp_ref, kc_ref, vp_ref, vc_ref, o_ref, kf_ref, vf_ref):
    W = WINDOW
    tq = q_ref.shape[1]
    blocks = tq // W
    tile = pl.program_id(1)
    kf_ref[0:W] = kp_ref[0]
    kf_ref[W:] = kc_ref[0]
    vf_ref[0:W] = vp_ref[0]
    vf_ref[W:] = vc_ref[0]
    low = lax.broadcasted_iota(jnp.int32, (2 * W, LANES), 1) < A_HEAD_DIM
    keep_low = jnp.where(low, 1.0, 0.0).astype(BF16)
    keep_high = jnp.where(low, 0.0, 1.0).astype(BF16)
    head_a = _head_a_lanes((2 * W, LANES))
    keep_a = head_a.astype(BF16)
    keep_b = (1.0 - head_a).astype(BF16)
    from_prev =(lax.broadcasted_iota(jnp.int32, (W, W), 1)
                 > lax.broadcasted_iota(jnp.int32, (W, W), 0))
    keep_prev = jnp.where(from_prev, 1.0, 0.0).astype(BF16)
    keep_cur = jnp.where(from_prev, 0.0, 1.0).astype(BF16)
    ones_cols = jnp.concatenate([keep_low, keep_high], axis=0)
    low_out = lax.broadcasted_iota(jnp.int32, (W, LANES), 1) < A_HEAD_DIM
    no_prev = jnp.where(tile > 0, 0.0, -jnp.inf)

    def scores(i, pair):
        kband = kf_ref[i * W:(i + 2) * W, pair * LANES:(pair + 1) * LANES]
        kcat = jnp.concatenate([kband * keep_a, kband * keep_b], axis=0)
        base = pair * A_GROUP * LANES
        qs = jnp.concatenate([q_ref[0, i * W:(i + 1) * W, base + g * LANES:base + (g + 1) * LANES]
                              for g in range(A_GROUP)], axis=0)
        return _dot_nt(qs, kcat)

    def finish(i, pair, sc):
        vband = vf_ref[i * W:(i + 2) * W, pair * LANES:(pair + 1) * LANES]
        vcat = jnp.concatenate([jnp.concatenate([vband * keep_low, vband * keep_high], axis=0),
                                ones_cols], axis=1)
        base = pair * A_GROUP * LANES
        slabs, sink_terms = [], []
        for g in range(A_GROUP):
            parts, terms = [], []
            for par in range(2):
                s_h = sc[g * W:(g + 1) * W, par * 2 * W:(par + 1) * 2 * W]
                s_prev = s_h[:, :W] + no_prev if i == 0 else s_h[:, :W]
                c = jnp.where(from_prev, s_prev, s_h[:, W:])
                m = jnp.max(c, axis=1, keepdims=True)
                e = jnp.exp(c - m).astype(BF16)
                parts += [e * keep_prev, e * keep_cur]
                terms.append(jnp.exp(sink_ref[(2 * pair + par) * A_GROUP + g] - m))
            slabs.append(jnp.concatenate(parts, axis=1))
            sink_terms.append(jnp.where(low_out, terms[0], terms[1]))
        r = _dot(jnp.concatenate(slabs, axis=0), vcat)
        for g in range(A_GROUP):
            rows = slice(g * W, (g + 1) * W)
            out = r[rows, :LANES] * (1.0 / (r[rows, LANES:] + sink_terms[g]))
            o_ref[0, i * W:(i + 1) * W, base + g * LANES:base + (g + 1) * LANES] = out.astype(BF16)

    units = [(i, pair) for i in range(blocks) for pair in range(A_KV_HEADS // 2)]
    sc = scores(*units[0])
    for u, unit in enumerate(units):
        nxt = scores(*units[u + 1]) if u + 1 < len(units) else None
        finish(*unit, sc)
        sc = nxt


def _attn_call(q, k, v, sinks):
    b, s, d = q.shape
    W = WINDOW
    tq = ATTN_QUERIES
    kw = k.shape[2]
    per = tq // W
    cur = lambda w: pl.BlockSpec((1, tq, w), lambda i, j, *_: (i, j, 0))
    prev = lambda w: pl.BlockSpec((1, W, w), lambda i, j, *_: (i, jnp.maximum(j * per - 1, 0), 0))
    return pl.pallas_call(
        _attn_body,
        grid_spec=pltpu.PrefetchScalarGridSpec(
            num_scalar_prefetch=1,
            grid=(b, s // tq),
            in_specs=[cur(d), prev(kw), cur(kw), prev(kw), cur(kw)],
            out_specs=cur(d),
            scratch_shapes=[pltpu.VMEM((tq + W, kw), BF16), pltpu.VMEM((tq + W, kw), BF16)]),
        out_shape=jax.ShapeDtypeStruct((b, s, d), BF16),
        compiler_params=pltpu.CompilerParams(
            dimension_semantics=("parallel", "parallel"), vmem_limit_bytes=VMEM_LIMIT),
        name="swa_attn",
    )(sinks, q, k, k, v, v)


def kernel(x, p, positions, a_w_in, a_b_igate, a_b_fgate, a_head_norm_g, a_w_out, kv_w, kv_b, b_w_q, b_b_q, b_sinks, b_w_o, b_b_o, mix_ln_g, mix_ln_b, mlp_w_up, mlp_w_down, mlp_ln_g, mlp_ln_b, ple_w_gate, ple_b_gate, ple_w_proj):
    B, S, D = x.shape
    N = B * S
    row = lambda v: v.reshape(1, -1).astype(F32)

    w_t = jnp.pad(a_w_in[0].T, ((0, M_AUG - 2 * M_HEADS), (0, 0))).astype(BF16)
    bgr = jnp.concatenate([a_b_igate[0], a_b_fgate[0]]).reshape(2 * M_HEADS, 1).astype(F32)
    gain_b = jnp.broadcast_to(a_head_norm_g[0].astype(F32)[:, None], (D, LANES))
    hg, (w_up, w_down, w_pgate, w_pproj, w_out0) = _mlstm_layer_call(
        x, w_t, bgr, gain_b, (mlp_w_up, mlp_w_down, ple_w_gate, ple_w_proj, a_w_out[0]))

    def ffn(i, a, xs, wo, bo):
        return _ffn_call(i, a, xs, p.reshape(DEPTH, N, PLE_DIM), wo, row(bo), mix_ln_g, mix_ln_b,
                         w_up, w_down, mlp_ln_g, mlp_ln_b, w_pgate, ple_b_gate, w_pproj)

    xs = ffn(0, hg.reshape(N, D), x.reshape(N, D), w_out0, jnp.zeros((D,), F32))

    half = ROPE_DIM // 2
    inv_freq = jnp.power(ROPE_THETA, -jnp.arange(half, dtype=F32) * (2.0 / ROPE_DIM))
    q_scale = A_HEAD_DIM ** -0.5
    def by_head(w, axis):
        split = w.reshape(*w.shape[:axis], A_KV_HEADS // 2, 2, A_GROUP, A_HEAD_DIM, *w.shape[axis + 1:])
        return jnp.swapaxes(split, axis + 1, axis + 2).reshape(w.shape)
    wq = _slab_lanes(by_head(b_w_q[0] * q_scale, 1)).astype(BF16)
    bq = row(_slab_lanes(by_head(b_b_q[0] * q_scale, 0)))
    kv_split = kv_w.shape[1] // 2
    wkv = jnp.concatenate([_slab_lanes(kv_w[:, :kv_split]), kv_w[:, kv_split:]], axis=1).astype(BF16)
    bkv = row(jnp.concatenate([_slab_lanes(kv_b[:kv_split]), kv_b[kv_split:]]))
    qr, kr, vr = _qkv_call(xs, positions.reshape(1, N), wq, bq, wkv, bkv,
                           inv_freq.reshape(half, 1), _rope_selector())
    att = _attn_call(qr.reshape(B, S, D), kr.reshape(B, S, -1), vr.reshape(B, S, -1), b_sinks[0].astype(F32))
    xs = ffn(1, att.reshape(N, D), xs, by_head(b_w_o[0], 0).astype(BF16), b_b_o[0])
    return xs.reshape(B, S, D)
```

```python
import functools

import jax
import jax.numpy as jnp
import numpy as np
from jax import lax
from jax.experimental import pallas as pl
from jax.experimental.pallas import tpu as pltpu

F32 = jnp.float32
BF16 = jnp.bfloat16

D_MODEL = 1024
DEPTH = 2
M_HEADS = 4
M_DV = D_MODEL // M_HEADS
M_DK = M_DV // 2
GATE_CAP = 15.0
A_HEAD_DIM = 64
A_Q_HEADS = D_MODEL // A_HEAD_DIM
A_KV_HEADS = 4
A_GROUP = A_Q_HEADS // A_KV_HEADS
WINDOW = 128
ROPE_DIM = A_HEAD_DIM // 4
ROPE_THETA = 500000.0
D_FF = 4 * D_MODEL
PLE_DIM = 256
LN_EPS = 1e-5
LOG2_E = 1.4426950408889634
DEEPNORM_ALPHA = (2 * DEPTH) ** 0.25

LANES = 128
VMEM_LIMIT = 56 * 1024 * 1024

FFN_TOKENS = 512
FFN_SUBTILES = 2
FF_CHUNK = 2048
PROJ_TOKENS = 1024
M_BLOCK = 256
M_STEP_TOKENS = 2048
M_AUG = 16
QKV_Q_GROUP = 256
ATTN_QUERIES = 2048

_NT = (((1,), (1,)), ((), ()))


def _dot(a, b):
    return jnp.dot(a, b, preferred_element_type=F32)


def _dot_nt(a, b):
    return lax.dot_general(a, b, _NT, preferred_element_type=F32)


def _resident(shape):
    zeros = (0,) * len(shape)
    return pl.BlockSpec(shape, lambda *_: zeros, pipeline_mode=pl.Buffered(1))


def _resident_part(block_shape, block_index):
    return pl.BlockSpec(block_shape, lambda *_: block_index, pipeline_mode=pl.Buffered(1))


def _layer_norm(y, g, b):
    mu = jnp.mean(y, axis=-1, keepdims=True)
    yc = y - mu
    var = jnp.mean(yc * yc, axis=-1, keepdims=True)
    return yc * lax.rsqrt(var + LN_EPS) * g + b


def _ffn_body(layer, a_ref, x_ref, p_ref, wo_ref, bo_ref, g1_ref, b1_ref, wup_ref, wdn_ref,
              g2_ref, b2_ref, wpg_ref, bpg_ref, wpp_ref, o_ref):
    sub = x_ref.shape[0] // FFN_SUBTILES
    rows = [slice(s * sub, (s + 1) * sub) for s in range(FFN_SUBTILES)]
    g1, b1, g2, b2, bpg = (r[layer:layer + 1, :] for r in (g1_ref, b1_ref, g2_ref, b2_ref, bpg_ref))

    def mlp(x1):
        x1b = x1.astype(BF16)
        acc = None
        for c in range(D_FF // FF_CHUNK):
            cols = slice(c * FF_CHUNK, (c + 1) * FF_CHUNK)
            h = jnp.maximum(_dot(x1b, wup_ref[:, cols]), 0.0)
            d = _dot((h * h).astype(BF16), wdn_ref[cols, :])
            acc = d if acc is None else acc + d
        return acc

    mix = [_dot(a_ref[r, :], wo_ref[...]) + bo_ref[...] for r in rows]
    x1 = [_layer_norm(DEEPNORM_ALPHA * x_ref[r, :] + m, g1, b1) for r, m in zip(rows, mix)]
    acc = [mlp(v) for v in x1]
    for r, v, a in zip(rows, x1, acc):
        x2 = _layer_norm(DEEPNORM_ALPHA * v + a, g2, b2)
        gate = jax.nn.sigmoid(_dot(x2.astype(BF16), wpg_ref[...]) + bpg)
        pe = _dot(p_ref[r, :].astype(BF16), wpp_ref[...])
        o_ref[r, :] = x2 + gate * pe


def _ffn_call(layer, a, x, p, wo, bo, g1, b1, wup, wdn, g2, b2, wpg, bpg, wpp):
    n, d = x.shape
    tm = FFN_TOKENS
    tok = lambda w: pl.BlockSpec((tm, w), lambda i: (i, 0))
    of_layer = lambda w: _resident_part((None,) + w.shape[1:], (layer,) + (0,) * (w.ndim - 1))
    return pl.pallas_call(
        functools.partial(_ffn_body, layer),
        grid=(n // tm,),
        in_specs=[tok(d), tok(d), pl.BlockSpec((None, tm, PLE_DIM), lambda i: (layer, i, 0)),
                  _resident(wo.shape), _resident(bo.shape), _resident(g1.shape), _resident(b1.shape),
                  of_layer(wup), of_layer(wdn), _resident(g2.shape), _resident(b2.shape),
                  of_layer(wpg), _resident(bpg.shape), of_layer(wpp)],
        out_specs=tok(d),
        out_shape=jax.ShapeDtypeStruct((n, d), F32),
        compiler_params=pltpu.CompilerParams(
            dimension_semantics=("arbitrary",), vmem_limit_bytes=VMEM_LIMIT),
        name="ffn",
    )(a, x, p, wo, bo, g1, b1, wup, wdn, g2, b2, wpg, bpg, wpp)


def _log_sigmoid(z):
    return jnp.minimum(z, 0.0) - jnp.log1p(jnp.exp(-jnp.abs(z)))


def _soft_cap(z):
    return GATE_CAP * jnp.tanh(z / GATE_CAP)


def _mlstm_layer_body(n_casts, x_ref, wqt_ref, wkt_ref, wvt_ref, wot_ref, wgr_ref, bgr_ref, gain_ref, *refs):
    cast_in, (o_ref, *cast_out), (c_ref, m_ref) = refs[:n_casts], refs[n_casts:2 * n_casts + 1], refs[2 * n_casts + 1:]

    @pl.when(pl.program_id(1) == 0)
    def _():
        c_ref[...] = jnp.zeros_like(c_ref)
        m_ref[...] = jnp.zeros_like(m_ref)

    for src, dst in zip(cast_in, cast_out):
        dst[...] = src[...].astype(BF16)

    L = M_BLOCK
    reps = L // LANES
    blocks = x_ref.shape[1] // L
    keep_diag = (lax.broadcasted_iota(jnp.int32, (LANES, LANES), 0)
                 <= lax.broadcasted_iota(jnp.int32, (LANES, LANES), 1))
    ones_rows = jnp.ones((M_AUG, L), BF16)
    zeros_q = jnp.zeros((LANES, LANES), F32)
    gain = jnp.concatenate([gain_ref[...]] * reps, axis=1)
    lane = lax.broadcasted_iota(jnp.int32, (2 * M_HEADS, L), 1)

    def scan(v, combine, identity):
        shift = 1
        while shift < L:
            v = combine(v, jnp.where(lane >= shift, pltpu.roll(v, shift, 1), identity))
            shift *= 2
        return v

    def project(blk):
        tok = slice(blk * L, (blk + 1) * L)
        out = {}
        xb = x_ref[0, tok, :].astype(BF16)
        xtb = x_ref[0, tok, :].T.astype(BF16)
        out["k"] = _dot_nt(xb, wkt_ref[...]).astype(BF16)
        out["qt"] = (_dot(wqt_ref[...], xtb) * (M_DK ** -0.5)).astype(BF16)
        z = _soft_cap(_dot(wgr_ref[...], xtb)[0:2 * M_HEADS] + bgr_ref[...])
        yield out
        log_i = jnp.concatenate([z[:M_HEADS]] * 2, axis=0) * LOG2_E
        log_f = _log_sigmoid(jnp.concatenate([z[M_HEADS:]] * 2, axis=0)) * LOG2_E
        out["ogt"] = (jax.nn.sigmoid(_dot(wot_ref[...], xtb)) * gain).astype(BF16)
        out["vt"] = _dot(wvt_ref[...], xtb).astype(BF16)
        out["b"] = scan(log_f, jnp.add, 0.0)
        g = log_i - out["b"]
        out["cm"] = scan(g, jnp.maximum, -jnp.inf)
        out["gc"] = jnp.concatenate([g, jnp.zeros((LANES - 2 * M_HEADS, L), F32)], axis=0).T
        yield out

    def early(pr, h):
        qt = pr["qt"][h * M_DK:(h + 1) * M_DK, :]
        kh = pr["k"][:, h * M_DK:(h + 1) * M_DK]
        lhs = jnp.concatenate([pr["vt"][h * M_DV:(h + 1) * M_DV, :], ones_rows], axis=0)
        m_prev = m_ref[h]
        m_row = jnp.maximum(pr["cm"][h:h + 1], m_prev)
        m_last = m_row[:, L - 1:L]
        g_b = jnp.broadcast_to(pr["gc"][:, h:h + 1], (L, LANES))
        c_old = c_ref[h]
        st = _dot(kh, qt)
        cq = _dot(c_old.astype(BF16), qt)
        kw = (kh.astype(F32) * jnp.exp2(g_b - m_last)).astype(BF16)
        c_ref[h] = jnp.exp2(m_prev - m_last) * c_old + _dot(lhs, kw)
        m_ref[h] = pr["b"][h:h + 1, L - 1:L] + m_last
        return lhs, pr["b"][h:h + 1], m_prev, m_row, g_b, st, cq

    def late(blk, pr, h, lhs, b_row, m_prev, m_row, g_b, st, cq):
        sd_rows = []
        for i in range(reps):
            src_rows = slice(i * LANES, (i + 1) * LANES)
            quads = []
            for j in range(reps):
                tgt_lanes = slice(j * LANES, (j + 1) * LANES)
                if j < i:
                    quads.append(zeros_q)
                    continue
                dq = jnp.exp2(g_b[src_rows] - m_row[:, tgt_lanes])
                if j == i:
                    dq = jnp.where(keep_diag, dq, 0.0)
                quads.append(st[src_rows, tgt_lanes] * dq)
            sd_rows.append(jnp.concatenate(quads, axis=1))
        sd = jnp.concatenate(sd_rows, axis=0).astype(BF16)
        tot = _dot(lhs, sd) + cq * jnp.exp2(m_prev - m_row)
        num, den = tot[:M_DV], tot[M_DV:M_DV + 1]
        dmax = jnp.maximum(jnp.abs(den), jnp.exp2(-(b_row + m_row)))
        mu = jnp.mean(num, axis=0, keepdims=True)
        hc = num - mu
        var = jnp.mean(hc * hc, axis=0, keepdims=True)
        hn = hc * lax.rsqrt(var + LN_EPS * dmax * dmax)
        og = pr["ogt"][h * M_DV:(h + 1) * M_DV, :].astype(F32)
        o_ref[0, blk * L:(blk + 1) * L, h * M_DV:(h + 1) * M_DV] = (hn * og).T.astype(BF16)

    def recur(blk, pr):
        carried = [early(pr, h) for h in range(M_HEADS)]
        yield
        for h in range(M_HEADS // 2):
            late(blk, pr, h, *carried[h])
        yield
        for h in range(M_HEADS // 2, M_HEADS):
            late(blk, pr, h, *carried[h])
        yield

    pr = None
    for out in project(0):
        pr = out
    for blk in range(blocks):
        nxt_stages = project(blk + 1) if blk + 1 < blocks else iter(())
        nxt = None
        for _ in recur(blk, pr):
            nxt = next(nxt_stages, nxt)
        for out in nxt_stages:
            nxt = out
        pr = nxt


def _mlstm_layer_call(x, w_t, bgr, gain_b, to_cast):
    b, s, d = x.shape
    t = M_STEP_TOKENS
    qk_w = M_HEADS * M_DK
    steps = s // t
    tok_major = pl.BlockSpec((1, t, d), lambda i, j: (i, j, 0))
    rows_of = lambda start, size: _resident_part((size, d), (start // size, 0))
    flat = [w.reshape(-1, w.shape[-1]) for w in to_cast]
    slices = [pl.BlockSpec((w.shape[0] // (b * steps), w.shape[1]), lambda i, j: (i * steps + j, 0)) for w in flat]
    outs = pl.pallas_call(
        functools.partial(_mlstm_layer_body, len(flat)),
        grid=(b, steps),
        in_specs=[tok_major, rows_of(0, qk_w), rows_of(qk_w, qk_w), rows_of(2 * qk_w, d),
                  rows_of(2 * qk_w + d, d), rows_of(2 * qk_w + 2 * d, M_AUG),
                  _resident(bgr.shape), _resident(gain_b.shape)] + slices,
        out_specs=[tok_major] + slices,
        out_shape=[jax.ShapeDtypeStruct((b, s, d), BF16)] + [jax.ShapeDtypeStruct(w.shape, BF16) for w in flat],
        scratch_shapes=[pltpu.VMEM((M_HEADS, M_DV + M_AUG, M_DK), F32),
                        pltpu.VMEM((M_HEADS, 1, 1), F32)],
        compiler_params=pltpu.CompilerParams(
            dimension_semantics=("parallel", "arbitrary"), vmem_limit_bytes=VMEM_LIMIT),
        name="mlstm_layer",
    )(x, w_t, w_t, w_t, w_t, w_t, bgr, gain_b, *flat)
    return outs[0], [o.reshape(w.shape) for o, w in zip(outs[1:], to_cast)]


_ROPE_HALF = ROPE_DIM // 2
_REST = A_HEAD_DIM - ROPE_DIM


def _slab_lanes(w):
    slabs = w.reshape(*w.shape[:-1], -1, 2, A_HEAD_DIM)
    a, b = slabs[..., 0, :], slabs[..., 1, :]
    t1 = lambda h: h[..., :_ROPE_HALF]
    t2 = lambda h: h[..., _ROPE_HALF:ROPE_DIM]
    rest = lambda h: h[..., ROPE_DIM:]
    return jnp.concatenate([t1(a), t1(b), rest(a), t2(a), t2(b), rest(b)], axis=-1).reshape(w.shape)


def _head_a_lanes(shape):
    lane = lax.broadcasted_iota(jnp.int32, shape, len(shape) - 1)
    in_t_group = jnp.where(lane % ROPE_DIM < _ROPE_HALF, 1.0, 0.0)
    in_rest = jnp.where(lane < A_HEAD_DIM + ROPE_DIM, 1.0, 0.0)
    return jnp.where(lane % A_HEAD_DIM < ROPE_DIM, in_t_group, in_rest)


def _rope_selector():
    sel = np.zeros((LANES, 2 * LANES), np.float32)
    for lane in range(LANES):
        freq, pos = lane % _ROPE_HALF, lane % A_HEAD_DIM
        if pos < ROPE_DIM:
            sel[freq, lane] = 1.0
            sel[_ROPE_HALF + freq, LANES + lane] = -1.0 if lane < A_HEAD_DIM else 1.0
        else:
            sel[2 * _ROPE_HALF, lane] = 1.0
    return jnp.asarray(np.concatenate([sel, sel], axis=0), BF16)


def _qkv_body(x_ref, pos_ref, wq_ref, bq_ref, wkv_ref, bkv_ref, invf_ref, sel_ref, q_ref, k_ref, v_ref):
    t = x_ref.shape[0]
    half = ROPE_DIM // 2
    xb = x_ref[...].astype(BF16)
    kv = _dot(xb, wkv_ref[...]) + bkv_ref[...]
    ang = invf_ref[...] * pos_ref[...].astype(F32)
    table = jnp.concatenate([jnp.cos(ang), jnp.sin(ang), jnp.ones_like(ang),
                             jnp.zeros((LANES - 3 * half, t), F32)], axis=0).T
    hi = table.astype(BF16)
    lo = (table - hi.astype(F32)).astype(BF16)
    coef = _dot(jnp.concatenate([hi, lo], axis=1), sel_ref[...])
    c_self, c_partner = coef[:, :LANES], coef[:, LANES:]

    def rope(z):
        return z * c_self + pltpu.roll(z, A_HEAD_DIM, 1) * c_partner

    group = QKV_Q_GROUP
    q0 = _dot(xb, wq_ref[:, :group]) + bq_ref[:, :group]
    kv_half = kv.shape[1] // 2
    for c in range(kv_half // LANES):
        k_ref[:, c * LANES:(c + 1) * LANES] = rope(kv[:, c * LANES:(c + 1) * LANES]).astype(BF16)
    v_ref[...] = kv[:, kv_half:].astype(BF16)
    for g in range(wq_ref.shape[1] // group):
        nxt = None
        if (g + 1) * group < wq_ref.shape[1]:
            cols = slice((g + 1) * group, (g + 2) * group)
            nxt = _dot(xb, wq_ref[:, cols]) + bq_ref[:, cols]
        for c in range(group // LANES):
            q_ref[:, g * group + c * LANES:g * group + (c + 1) * LANES] = rope(
                q0[:, c * LANES:(c + 1) * LANES]).astype(BF16)
        q0 = nxt


def _qkv_call(x, pos, wq, bq, wkv, bkv, invf, sel):
    n, d = x.shape
    t = PROJ_TOKENS
    kw = wkv.shape[1] // 2
    tok = lambda w: pl.BlockSpec((t, w), lambda i: (i, 0))
    return pl.pallas_call(
        _qkv_body,
        grid=(n // t,),
        in_specs=[tok(d), pl.BlockSpec((1, t), lambda i: (0, i)), _resident(wq.shape), _resident(bq.shape),
                  _resident(wkv.shape), _resident(bkv.shape), _resident(invf.shape), _resident(sel.shape)],
        out_specs=[tok(d), tok(kw), tok(kw)],
        out_shape=[jax.ShapeDtypeStruct((n, d), BF16),
                   jax.ShapeDtypeStruct((n, kw), BF16),
                   jax.ShapeDtypeStruct((n, kw), BF16)],
        compiler_params=pltpu.CompilerParams(
            dimension_semantics=("parallel",), vmem_limit_bytes=VMEM_LIMIT),
        name="qkv_rope",
    )(x, pos, wq, bq, wkv, bkv, invf, sel)


def _attn_body(sink_ref, q_ref, kp_ref, kc_ref, vp_ref, vc_ref, o_ref, kf_ref, vf_ref):
    W = WINDOW
    tq = q_ref.shape[1]
    blocks = tq // W
    tile = pl.program_id(1)
    kf_ref[0:W] = kp_ref[0]
    kf_ref[W:] = kc_ref[0]
    vf_ref[0:W] = vp_ref[0]
    vf_ref[W:] = vc_ref[0]
    low = lax.broadcasted_iota(jnp.int32, (2 * W, LANES), 1) < A_HEAD_DIM
    keep_low = jnp.where(low, 1.0, 0.0).astype(BF16)
    keep_high = jnp.where(low, 0.0, 1.0).astype(BF16)
    head_a = _head_a_lanes((2 * W, LANES))
    keep_a = head_a.astype(BF16)
    keep_b = (1.0 - head_a).astype(BF16)
    from_prev =(lax.broadcasted_iota(jnp.int32, (W, W), 1)
                 > lax.broadcasted_iota(jnp.int32, (W, W), 0))
    keep_prev = jnp.where(from_prev, 1.0, 0.0).astype(BF16)
    keep_cur = jnp.where(from_prev, 0.0, 1.0).astype(BF16)
    ones_cols = jnp.concatenate([keep_low, keep_high], axis=0)
    low_out = lax.broadcasted_iota(jnp.int32, (W, LANES), 1) < A_HEAD_DIM
    no_prev = jnp.where(tile > 0, 0.0, -jnp.inf)

    def scores(i, pair):
        kband = kf_ref[i * W:(i + 2) * W, pair * LANES:(pair + 1) * LANES]
        kcat = jnp.concatenate([kband * keep_a, kband * keep_b], axis=0)
        base = pair * A_GROUP * LANES
        qs = jnp.concatenate([q_ref[0, i * W:(i + 1) * W, base + g * LANES:base + (g + 1) * LANES]
                              for g in range(A_GROUP)], axis=0)
        return _dot_nt(qs, kcat)

    def finish(i, pair, sc):
        vband = vf_ref[i * W:(i + 2) * W, pair * LANES:(pair + 1) * LANES]
        vcat = jnp.concatenate([jnp.concatenate([vband * keep_low, vband * keep_high], axis=0),
                                ones_cols], axis=1)
        base = pair * A_GROUP * LANES
        slabs, sink_terms = [], []
        for g in range(A_GROUP):
            parts, terms = [], []
            for par in range(2):
                s_h = sc[g * W:(g + 1) * W, par * 2 * W:(par + 1) * 2 * W]
                s_prev = s_h[:, :W] + no_prev if i == 0 else s_h[:, :W]
                c = jnp.where(from_prev, s_prev, s_h[:, W:])
                m = jnp.max(c, axis=1, keepdims=True)
                e = jnp.exp(c - m).astype(BF16)
                parts += [e * keep_prev, e * keep_cur]
                terms.append(jnp.exp(sink_ref[(2 * pair + par) * A_GROUP + g] - m))
            slabs.append(jnp.concatenate(parts, axis=1))
            sink_terms.append(jnp.where(low_out, terms[0], terms[1]))
        r = _dot(jnp.concatenate(slabs, axis=0), vcat)
        for g in range(A_GROUP):
            rows = slice(g * W, (g + 1) * W)
            out = r[rows, :LANES] * (1.0 / (r[rows, LANES:] + sink_terms[g]))
            o_ref[0, i * W:(i + 1) * W, base + g * LANES:base + (g + 1) * LANES] = out.astype(BF16)

    units = [(i, pair) for i in range(blocks) for pair in range(A_KV_HEADS // 2)]
    sc = scores(*units[0])
    for u, unit in enumerate(units):
        nxt = scores(*units[u + 1]) if u + 1 < len(units) else None
        finish(*unit, sc)
        sc = nxt


def _attn_call(q, k, v, sinks):
    b, s, d = q.shape
    W = WINDOW
    tq = ATTN_QUERIES
    kw = k.shape[2]
    per = tq // W
    cur = lambda w: pl.BlockSpec((1, tq, w), lambda i, j, *_: (i, j, 0))
    prev = lambda w: pl.BlockSpec((1, W, w), lambda i, j, *_: (i, jnp.maximum(j * per - 1, 0), 0))
    return pl.pallas_call(
        _attn_body,
        grid_spec=pltpu.PrefetchScalarGridSpec(
            num_scalar_prefetch=1,
            grid=(b, s // tq),
            in_specs=[cur(d), prev(kw), cur(kw), prev(kw), cur(kw)],
            out_specs=cur(d),
            scratch_shapes=[pltpu.VMEM((tq + W, kw), BF16), pltpu.VMEM((tq + W, kw), BF16)]),
        out_shape=jax.ShapeDtypeStruct((b, s, d), BF16),
        compiler_params=pltpu.CompilerParams(
            dimension_semantics=("parallel", "parallel"), vmem_limit_bytes=VMEM_LIMIT),
        name="swa_attn",
    )(sinks, q, k, k, v, v)


def kernel(x, p, positions, a_w_in, a_b_igate, a_b_fgate, a_head_norm_g, a_w_out, kv_w, kv_b, b_w_q, b_b_q, b_sinks, b_w_o, b_b_o, mix_ln_g, mix_ln_b, mlp_w_up, mlp_w_down, mlp_ln_g, mlp_ln_b, ple_w_gate, ple_b_gate, ple_w_proj):
    B, S, D = x.shape
    N = B * S
    row = lambda v: v.reshape(1, -1).astype(F32)

    w_t = jnp.pad(a_w_in[0].T, ((0, M_AUG - 2 * M_HEADS), (0, 0))).astype(BF16)
    bgr = jnp.concatenate([a_b_igate[0], a_b_fgate[0]]).reshape(2 * M_HEADS, 1).astype(F32)
    gain_b = jnp.broadcast_to(a_head_norm_g[0].astype(F32)[:, None], (D, LANES))
    hg, (w_up, w_down, w_pgate, w_pproj, w_out0) = _mlstm_layer_call(
        x, w_t, bgr, gain_b, (mlp_w_up, mlp_w_down, ple_w_gate, ple_w_proj, a_w_out[0]))

    def ffn(i, a, xs, wo, bo):
        return _ffn_call(i, a, xs, p.reshape(DEPTH, N, PLE_DIM), wo, row(bo), mix_ln_g, mix_ln_b,
                         w_up, w_down, mlp_ln_g, mlp_ln_b, w_pgate, ple_b_gate, w_pproj)

    xs = ffn(0, hg.reshape(N, D), x.reshape(N, D), w_out0, jnp.zeros((D,), F32))

    half = ROPE_DIM // 2
    inv_freq = jnp.power(ROPE_THETA, -jnp.arange(half, dtype=F32) * (2.0 / ROPE_DIM))
    q_scale = A_HEAD_DIM ** -0.5
    def by_head(w, axis):
        split = w.reshape(*w.shape[:axis], A_KV_HEADS // 2, 2, A_GROUP, A_HEAD_DIM, *w.shape[axis + 1:])
        return jnp.swapaxes(split, axis + 1, axis + 2).reshape(w.shape)
    wq = _slab_lanes(by_head(b_w_q[0] * q_scale, 1)).astype(BF16)
    bq = row(_slab_lanes(by_head(b_b_q[0] * q_scale, 0)))
    kv_split = kv_w.shape[1] // 2
    wkv = jnp.concatenate([_slab_lanes(kv_w[:, :kv_split]), kv_w[:, kv_split:]], axis=1).astype(BF16)
    bkv = row(jnp.concatenate([_slab_lanes(kv_b[:kv_split]), kv_b[kv_split:]]))
    qr, kr, vr = _qkv_call(xs, positions.reshape(1, N), wq, bq, wkv, bkv,
                           inv_freq.reshape(half, 1), _rope_selector())
    att = _attn_call(qr.reshape(B, S, D), kr.reshape(B, S, -1), vr.reshape(B, S, -1), b_sinks[0].astype(F32))
    xs = ffn(1, att.reshape(N, D), xs, by_head(b_w_o[0], 0).astype(BF16), b_b_o[0])
    return xs.reshape(B, S, D)
```
